```python
import jax, jax.numpy as jnp
from jax import lax
import numpy as np

D_MODEL = 1024
BATCH = 16
SEQ = 2048
DEPTH = 2

D_MIX = D_MODEL
ATT_HEADS = 8
ATT_HEAD_DIM = 64
ATT_WIDTH = ATT_HEADS * ATT_HEAD_DIM
MLSTM_HEADS = 4
MLSTM_HEAD_DIM = 128
MLSTM_WIDTH = MLSTM_HEADS * MLSTM_HEAD_DIM
Q_RANK = 256
N_IDX_HEADS = 8
IDX_DIM = 64
INDEX_TOPK = 256
Q_BLOCK = 128
MLSTM_CHUNK = 64
CONV_WIDTH = 4
FFN_HIDDEN = -(-8 * D_MODEL // (3 * 256)) * 256
ROPE_THETA = 10000.0
EPS = 1e-6
IN_SPLITS = (Q_RANK, ATT_WIDTH, ATT_WIDTH, IDX_DIM, N_IDX_HEADS,
             MLSTM_WIDTH, MLSTM_WIDTH, MLSTM_WIDTH, MLSTM_WIDTH, MLSTM_HEADS, MLSTM_HEADS)
IN_COLS = sum(IN_SPLITS)

kernel_name = "hybrid_dsa_mlstm_sandwich_adaln_block"


def rms_norm(x, w):
    x32 = x.astype(jnp.float32)
    y = x32 * lax.rsqrt(jnp.mean(x32 * x32, axis=-1, keepdims=True) + EPS)
    return (y * w.astype(jnp.float32)).astype(x.dtype)


def rope(x, positions):
    d = x.shape[-1]
    inv_freq = ROPE_THETA ** (-jnp.arange(0, d, 2, dtype=jnp.float32) / d)
    ang = positions.astype(jnp.float32)[..., None] * inv_freq
    cos = jnp.cos(ang)[:, :, None, :]
    sin = jnp.sin(ang)[:, :, None, :]
    x1, x2 = jnp.split(x.astype(jnp.float32), 2, axis=-1)
    return jnp.concatenate([x1 * cos - x2 * sin, x2 * cos + x1 * sin], axis=-1).astype(x.dtype)


def causal_dwconv(x, w, b):
    ch = x.shape[-1]
    y = lax.conv_general_dilated(x, w[:, None, :].astype(x.dtype), window_strides=(1,),
                                 padding=[(CONV_WIDTH - 1, 0)],
                                 dimension_numbers=('NWC', 'WIO', 'NWC'),
                                 feature_group_count=ch)
    return y + b.astype(x.dtype)


def dsa_attention(q, k, v, q_idx, k_idx, w_idx):
    def one_seq(args):
        q_, k_, v_, qi_, ki_, wi_ = args
        seq = q_.shape[0]
        n_sel = min(INDEX_TOPK, seq // 4)
        key_pos = jnp.arange(seq)
        ki32 = ki_.astype(jnp.float32)
        scale = ATT_HEAD_DIM ** -0.5

        def block(i):
            start = i * Q_BLOCK
            t = start + jnp.arange(Q_BLOCK)
            qb = lax.dynamic_slice_in_dim(q_, start, Q_BLOCK, 0)
            qib = lax.dynamic_slice_in_dim(qi_, start, Q_BLOCK, 0).astype(jnp.float32)
            wib = lax.dynamic_slice_in_dim(wi_, start, Q_BLOCK, 0).astype(jnp.float32)
            idx_logits = jnp.einsum('thd,sd->ths', qib, ki32)
            idx_score = jnp.einsum('ths,th->ts', jax.nn.relu(idx_logits), wib)
            idx_score = jnp.where(key_pos[None, :] <= t[:, None], idx_score, -jnp.inf)
            _, sel = lax.top_k(idx_score, n_sel)
            valid = sel <= t[:, None]
            k_sel = k_[sel]
            v_sel = v_[sel]
            logits = jnp.einsum('thd,tkhd->thk', qb, k_sel).astype(jnp.float32) * scale
            logits = jnp.where(valid[:, None, :], logits, -jnp.inf)
            p = jax.nn.softmax(logits, axis=-1)
            return jnp.einsum('thk,tkhd->thd', p.astype(v_.dtype), v_sel)

        out = lax.map(block, jnp.arange(seq // Q_BLOCK))
        return out.reshape((seq,) + q_.shape[1:])

    return lax.map(one_seq, (q, k, v, q_idx, k_idx, w_idx))


def mlstm_chunkwise(q, k, v, ig, lf):
    b_, nh, seq, dk = q.shape
    dv = v.shape[-1]
    nc = seq // MLSTM_CHUNK

    def chunks(a):
        return jnp.moveaxis(a.reshape(a.shape[:2] + (nc, MLSTM_CHUNK) + a.shape[3:]), 2, 0)

    qc, kc, vc, igc, lfc = chunks(q), chunks(k), chunks(v), chunks(ig), chunks(lf)
    bc = jnp.cumsum(lfc, axis=-1)
    causal = jnp.tril(jnp.ones((MLSTM_CHUNK, MLSTM_CHUNK), dtype=bool))

    def step(carry, inp):
        c_s, n_s, m_s = carry
        q_, k_, v_, ig_, b_c = inp
        d_mat = b_c[..., :, None] - b_c[..., None, :] + ig_[..., None, :]
        d_mat = jnp.where(causal, d_mat, -jnp.inf)
        inter = b_c + m_s[..., None]
        m_row = jnp.maximum(inter, jnp.max(d_mat, axis=-1))
        w_intra = jnp.exp(d_mat - m_row[..., None])
        w_inter = jnp.exp(inter - m_row)
        s_mat = jnp.einsum('bhtd,bhsd->bhts', q_, k_) * w_intra
        num = (jnp.einsum('bhts,bhsv->bhtv', s_mat, v_)
               + w_inter[..., None] * jnp.einsum('bhtd,bhdv->bhtv', q_, c_s))
        den = jnp.sum(s_mat, axis=-1) + w_inter * jnp.einsum('bhtd,bhd->bht', q_, n_s)
        h = num / jnp.maximum(jnp.abs(den), jnp.exp(-m_row))[..., None]
        total = b_c[..., -1]
        w_log = total[..., None] - b_c + ig_
        m_new = jnp.maximum(total + m_s, jnp.max(w_log, axis=-1))
        decay = jnp.exp(total + m_s - m_new)
        w_state = jnp.exp(w_log - m_new[..., None])
        c_new = decay[..., None, None] * c_s + jnp.einsum('bhs,bhsd,bhsv->bhdv', w_state, k_, v_)
        n_new = decay[..., None] * n_s + jnp.einsum('bhs,bhsd->bhd', w_state, k_)
        return (c_new, n_new, m_new), h

    init = (jnp.zeros((b_, nh, dk, dv), jnp.float32), jnp.zeros((b_, nh, dk), jnp.float32),
            jnp.zeros((b_, nh), jnp.float32))
    _, hs = lax.scan(step, init, (qc, kc, vc, igc, bc))
    return jnp.moveaxis(hs, 0, 2).reshape(b_, nh, seq, dv)


def hybrid_mixer(h, positions, w_in, q_latent_norm, w_q_up, w_qidx_up, conv_w, conv_b,
                 b_igate, b_fgate, attn_out_norm, mlstm_out_norm, w_out):
    b_, seq, _ = h.shape
    split_pts = [int(s) for s in np.cumsum(IN_SPLITS)[:-1]]
    (cq, k_att, v_att, k_idx, w_idx,
     q_m, k_m, v_m, o_m, i_m, f_m) = jnp.split(h @ w_in, split_pts, axis=-1)

    cq = rms_norm(cq, q_latent_norm)
    q = rope((cq @ w_q_up).reshape(b_, seq, ATT_HEADS, ATT_HEAD_DIM), positions)
    q_idx = rope((cq @ w_qidx_up).reshape(b_, seq, N_IDX_HEADS, IDX_DIM), positions)
    k = rope(k_att.reshape(b_, seq, ATT_HEADS, ATT_HEAD_DIM), positions)
    v = v_att.reshape(b_, seq, ATT_HEADS, ATT_HEAD_DIM)
    k_idx = rope(k_idx[:, :, None, :], positions)[:, :, 0, :]
    w_idx = w_idx * (N_IDX_HEADS * IDX_DIM) ** -0.5
    att = dsa_attention(q, k, v, q_idx, k_idx, w_idx)
    att = rms_norm(att, attn_out_norm.reshape(ATT_HEADS, ATT_HEAD_DIM)).reshape(b_, seq, ATT_WIDTH)

    qk = jax.nn.silu(causal_dwconv(jnp.concatenate([q_m, k_m], axis=-1), conv_w, conv_b))
    q_m, k_m = jnp.split(qk, 2, axis=-1)

    def heads(a):
        return a.reshape(b_, seq, MLSTM_HEADS, MLSTM_HEAD_DIM).transpose(0, 2, 1, 3).astype(jnp.float32)

    ig = (i_m + b_igate).astype(jnp.float32).transpose(0, 2, 1)
    lf = jax.nn.log_sigmoid((f_m + b_fgate).astype(jnp.float32)).transpose(0, 2, 1)
    hm = mlstm_chunkwise(heads(q_m) * MLSTM_HEAD_DIM ** -0.5, heads(k_m), heads(v_m), ig, lf)
    hm = hm.transpose(0, 2, 1, 3).astype(h.dtype)
    hm = rms_norm(hm, mlstm_out_norm.reshape(MLSTM_HEADS, MLSTM_HEAD_DIM))
    hm = (jax.nn.sigmoid(o_m).reshape(b_, seq, MLSTM_HEADS, MLSTM_HEAD_DIM) * hm).reshape(b_, seq, MLSTM_WIDTH)

    return jnp.concatenate([att, hm], axis=-1) @ w_out


def swiglu(h, w_gate_up, w_down):
    g, u = jnp.split(h @ w_gate_up, 2, axis=-1)
    return (jax.nn.silu(g) * u) @ w_down


def setup_inputs(seed: int = 0) -> dict:
    key = jax.random.key(seed)
    ks = jax.random.split(key, 24)
    L = DEPTH

    def nrm(k, shape, scale):
        return jax.random.normal(k, shape, jnp.float32) * scale

    def gain(k, shape):
        return 1.0 + 0.05 * jax.random.normal(k, shape, jnp.float32)

    x = nrm(ks[0], (BATCH, SEQ, D_MODEL), 1.0)
    c = nrm(ks[1], (BATCH, D_MODEL), 1.0)
    offsets = jax.random.randint(ks[2], (BATCH, 1), 0, 1024, dtype=jnp.int32)
    positions = (offsets + jnp.arange(SEQ, dtype=jnp.int32)[None, :]).astype(jnp.int32)
    b_fgate = (jnp.linspace(3.0, 6.0, MLSTM_HEADS, dtype=jnp.float32)[None, :]
               + nrm(ks[14], (L, MLSTM_HEADS), 0.1))
    return {
        "x": x,
        "c": c,
        "positions": positions,
        "w_mod": nrm(ks[3], (L, D_MODEL, 6 * D_MODEL), 0.5 * D_MODEL ** -0.5),
        "b_mod": nrm(ks[4], (L, 6 * D_MODEL), 0.01),
        "mix_norm_pre": gain(ks[5], (L, D_MODEL)),
        "mix_norm_post": gain(ks[6], (L, D_MODEL)),
        "w_in": nrm(ks[7], (L, D_MODEL, IN_COLS), D_MODEL ** -0.5),
        "q_latent_norm": gain(ks[8], (L, Q_RANK)),
        "w_q_up": nrm(ks[9], (L, Q_RANK, ATT_WIDTH), Q_RANK ** -0.5),
        "w_qidx_up": nrm(ks[10], (L, Q_RANK, N_IDX_HEADS * IDX_DIM), Q_RANK ** -0.5),
        "conv_w": nrm(ks[11], (L, CONV_WIDTH, 2 * MLSTM_WIDTH), CONV_WIDTH ** -0.5),
        "conv_b": nrm(ks[12], (L, 2 * MLSTM_WIDTH), 0.01),
        "b_igate": nrm(ks[13], (L, MLSTM_HEADS), 0.1),
        "b_fgate": b_fgate,
        "attn_out_norm": gain(ks[15], (L, ATT_WIDTH)),
        "mlstm_out_norm": gain(ks[16], (L, MLSTM_WIDTH)),
        "w_out": nrm(ks[17], (L, D_MIX, D_MODEL), D_MIX ** -0.5),
        "ffn_norm_pre": gain(ks[18], (L, D_MODEL)),
        "ffn_norm_post": gain(ks[19], (L, D_MODEL)),
        "w_gate_up": nrm(ks[20], (L, D_MODEL, 2 * FFN_HIDDEN), D_MODEL ** -0.5),
        "w_down": nrm(ks[21], (L, FFN_HIDDEN, D_MODEL), FFN_HIDDEN ** -0.5),
    }


def reference(x, c, positions, w_mod, b_mod, mix_norm_pre, mix_norm_post, w_in, q_latent_norm,
              w_q_up, w_qidx_up, conv_w, conv_b, b_igate, b_fgate, attn_out_norm, mlstm_out_norm,
              w_out, ffn_norm_pre, ffn_norm_post, w_gate_up, w_down):
    c_act = jax.nn.silu(c)
    for l in range(DEPTH):
        mod = c_act @ w_mod[l] + b_mod[l]
        sh_m, sc_m, g_m, sh_f, sc_f, g_f = jnp.split(mod, 6, axis=-1)
        h = rms_norm(x, mix_norm_pre[l]) * (1.0 + sc_m[:, None, :]) + sh_m[:, None, :]
        y = hybrid_mixer(h, positions, w_in[l], q_latent_norm[l], w_q_up[l], w_qidx_up[l],
                         conv_w[l], conv_b[l], b_igate[l], b_fgate[l], attn_out_norm[l],
                         mlstm_out_norm[l], w_out[l])
        x = x + g_m[:, None, :] * rms_norm(y, mix_norm_post[l])
        h = rms_norm(x, ffn_norm_pre[l]) * (1.0 + sc_f[:, None, :]) + sh_f[:, None, :]
        y = swiglu(h, w_gate_up[l], w_down[l])
        x = x + g_f[:, None, :] * rms_norm(y, ffn_norm_post[l])
    return x
```

```python
import functools

import numpy as np
import jax
import jax.numpy as jnp
from jax import lax
from jax.experimental import pallas as pl
from jax.experimental.pallas import tpu as pltpu

EPS = 1e-6
ROPE_THETA = 10000.0
ATT_HEADS = 8
HEAD_DIM = 64
ATT_WIDTH = ATT_HEADS * HEAD_DIM
Q_RANK = 256
IDX_HEADS = 8
ML_HEADS = 4
ML_DIM = 128
ML_WIDTH = ML_HEADS * ML_DIM
CONV_W = 4
TOPK = 256
QB = 128
LANE = 128
ML_CHUNK = 256
IN_SPLITS = (Q_RANK, ATT_WIDTH, ATT_WIDTH, HEAD_DIM, IDX_HEADS,
             ML_WIDTH, ML_WIDTH, ML_WIDTH, ML_WIDTH, ML_HEADS, ML_HEADS)

OFF_CQ = 0
OFF_K = OFF_CQ + Q_RANK
OFF_V = OFF_K + ATT_WIDTH
OFF_MISC = OFF_V + ATT_WIDTH
OFF_QKM = OFF_MISC + LANE
OFF_VM = OFF_QKM + 2 * ML_WIDTH
OFF_OM = OFF_VM + ML_WIDTH
IN_COLS_PADDED = OFF_OM + ML_WIDTH
MISC_W = HEAD_DIM
MISC_I = MISC_W + IDX_HEADS
MISC_F = MISC_I + ML_HEADS
MISC_END = MISC_F + ML_HEADS

VMEM_LIMIT = 56 * 1024 * 1024
NEG_BIG = -1e30
KEY_POS_INF = 0x7F800000
KEY_NEG_INF = (0xFF800000 - (1 << 32)) ^ 0x7FFFFFFF

bf16 = jnp.bfloat16
f32 = jnp.float32


def _rms(x, w):
    return x * lax.rsqrt(jnp.mean(x * x, axis=-1, keepdims=True) + EPS) * w


def _dot(a, b, precision=None):
    return jnp.dot(a, b, preferred_element_type=f32, precision=precision)


def _rope(x, cos, sin_signed):
    n = x.shape[1]
    lane = lax.broadcasted_iota(jnp.int32, x.shape, 1)
    low = (lane & (HEAD_DIM // 2)) == 0
    rot = jnp.where(low, pltpu.roll(x, n - HEAD_DIM // 2, 1), pltpu.roll(x, HEAD_DIM // 2, 1))
    return x * cos + rot * sin_signed


def _mod_kernel(c_ref, w_ref, b_ref, o_ref):
    c = c_ref[...]
    c_act = c * jax.nn.sigmoid(c)
    o_ref[0] = _dot(c_act, w_ref[0], precision=lax.Precision.HIGHEST) + b_ref[0]


def _modulation(c, w_mod, b_mod):
    depth, d, d6 = w_mod.shape
    b = c.shape[0]
    nj = d6 // d
    return pl.pallas_call(
        _mod_kernel,
        out_shape=jax.ShapeDtypeStruct((depth, b, d6), f32),
        grid=(depth, nj),
        in_specs=[pl.BlockSpec((b, d), lambda l, j: (0, 0)),
                  pl.BlockSpec((1, d, d), lambda l, j: (l, 0, j)),
                  pl.BlockSpec((1, 1, d), lambda l, j: (l, 0, j))],
        out_specs=pl.BlockSpec((1, b, d), lambda l, j: (l, 0, j)),
        name="modulation",
    )(c, w_mod, b_mod.reshape(depth, 1, d6))


def _rope_table_kernel(pos_ref, invf_ref, sign_ref, cos_ref, sin_ref):
    ang = pos_ref[0].astype(f32) * invf_ref[...]
    cos_ref[0] = jnp.cos(ang)
    sin_ref[0] = jnp.sin(ang) * sign_ref[...]


def _rope_tables(positions):
    b, s = positions.shape
    ts = min(s, 512)
    inv_freq = ROPE_THETA ** (-jnp.arange(0, HEAD_DIM, 2, dtype=f32) / HEAD_DIM)
    invf = jnp.tile(inv_freq, LANE // (HEAD_DIM // 2)).reshape(1, LANE)
    sign = np.where((np.arange(LANE) % HEAD_DIM) < HEAD_DIM // 2, -1.0, 1.0).astype(np.float32).reshape(1, LANE)
    tab = jax.ShapeDtypeStruct((b, s, LANE), f32)
    return pl.pallas_call(
        _rope_table_kernel,
        out_shape=(tab, tab),
        grid=(b, s // ts),
        in_specs=[pl.BlockSpec((1, ts, 1), lambda i, j: (i, j, 0)),
                  pl.BlockSpec((1, LANE), lambda i, j: (0, 0)),
                  pl.BlockSpec((1, LANE), lambda i, j: (0, 0))],
        out_specs=(pl.BlockSpec((1, ts, LANE), lambda i, j: (i, j, 0)),
                   pl.BlockSpec((1, ts, LANE), lambda i, j: (i, j, 0))),
        name="rope_tables",
    )(positions.reshape(b, s, 1), invf, jnp.asarray(sign))


def _in_proj_kernel(x_ref, mod_ref, nw_ref, win_ref, qln_ref, wq_ref, cw_ref, cb_ref, gb_ref, cos_ref, sin_ref,
                    q_ref, qi_ref, kt_ref, v_ref, kit_ref, misc_ref, gt_ref, qm_ref, kmt_ref, vm_ref, om_ref,
                    buf_ref):
    j = pl.program_id(1)
    tm = x_ref.shape[1]
    x = x_ref[0]
    sh = mod_ref[0, 0:1, :]
    sc = mod_ref[0, 1:2, :]
    hb = (_rms(x, nw_ref[...]) * (1.0 + sc) + sh).astype(bf16)

    cos = cos_ref[0]
    sin = sin_ref[0]
    reps = ATT_WIDTH // LANE
    cos_w = jnp.concatenate([cos] * reps, axis=1)
    sin_w = jnp.concatenate([sin] * reps, axis=1)

    cq = _dot(hb, win_ref[:, OFF_CQ:OFF_CQ + Q_RANK])
    cqn = _rms(cq, qln_ref[...]).astype(bf16)
    qq = _dot(cqn, wq_ref[...])
    q = _rope(qq[:, :ATT_WIDTH], cos_w, sin_w) * (HEAD_DIM ** -0.5)
    q_ref[0] = q.astype(bf16)
    qi_ref[0] = _rope(qq[:, ATT_WIDTH:], cos_w, sin_w).astype(bf16)
    k = _rope(_dot(hb, win_ref[:, OFF_K:OFF_K + ATT_WIDTH]), cos_w, sin_w)
    kt_ref[0] = k.T.astype(bf16)
    v_ref[0] = _dot(hb, win_ref[:, OFF_V:OFF_V + ATT_WIDTH]).astype(bf16)

    m = _dot(hb, win_ref[:, OFF_MISC:OFF_MISC + LANE])
    lane = lax.broadcasted_iota(jnp.int32, m.shape, 1)
    roped = _rope(m, cos, sin)
    biased = m + gb_ref[...]
    logsig = jnp.minimum(biased, 0.0) - jnp.log1p(jnp.exp(-jnp.abs(biased)))
    w_scale = (IDX_HEADS * HEAD_DIM) ** -0.5
    misc = jnp.where(lane < MISC_W, roped,
                     jnp.where(lane < MISC_I, m * w_scale,
                               jnp.where(lane < MISC_F, biased,
                                         jnp.where(lane < MISC_END, logsig, 0.0))))
    misc_ref[0] = misc
    misc_t = misc.T
    kit_ref[0] = misc_t[0:HEAD_DIM, :].astype(bf16)
    gt_ref[0] = misc_t[MISC_I:MISC_END, :]

    pre = _dot(hb, win_ref[:, OFF_QKM:OFF_QKM + 2 * ML_WIDTH])

    @pl.when(j == 0)
    def _():
        buf_ref[0:8, :] = jnp.zeros((8, 2 * ML_WIDTH), f32)

    buf_ref[8:8 + tm, :] = pre
    acc = jnp.broadcast_to(cb_ref[...], pre.shape)
    for t in range(CONV_W):
        acc = acc + cw_ref[t:t + 1, :] * buf_ref[8 - (CONV_W - 1) + t:8 - (CONV_W - 1) + t + tm, :]
    buf_ref[0:8, :] = buf_ref[tm:tm + 8, :]
    qk = acc * jax.nn.sigmoid(acc)
    qm_ref[0] = (qk[:, :ML_WIDTH] * (ML_DIM ** -0.5)).astype(bf16)
    kmt_ref[0] = qk[:, ML_WIDTH:].T.astype(bf16)
    vm_ref[0] = _dot(hb, win_ref[:, OFF_VM:OFF_VM + ML_WIDTH]).astype(bf16)
    om_ref[0] = _dot(hb, win_ref[:, OFF_OM:OFF_OM + ML_WIDTH])


def _in_proj(x, mod_l, nw, win, qln, wq, cw, cb, gb, cos, sin, tm):
    b, s, d = x.shape
    nt = s // tm

    def tok(w):
        return pl.BlockSpec((1, tm, w), lambda i, j: (i, j, 0))

    def tok_t(r):
        return pl.BlockSpec((1, r, tm), lambda i, j: (i, 0, j))

    def full(a):
        return pl.BlockSpec(a.shape, lambda i, j: (0,) * a.ndim)

    sd = jax.ShapeDtypeStruct
    out_shape = (sd((b, s, ATT_WIDTH), bf16), sd((b, s, ATT_WIDTH), bf16), sd((b, ATT_WIDTH, s), bf16),
                 sd((b, s, ATT_WIDTH), bf16), sd((b, HEAD_DIM, s), bf16), sd((b, s, LANE), f32),
                 sd((b, 2 * ML_HEADS, s), f32), sd((b, s, ML_WIDTH), bf16), sd((b, ML_WIDTH, s), bf16),
                 sd((b, s, ML_WIDTH), bf16), sd((b, s, ML_WIDTH), f32))
    out_specs = (tok(ATT_WIDTH), tok(ATT_WIDTH), tok_t(ATT_WIDTH), tok(ATT_WIDTH), tok_t(HEAD_DIM), tok(LANE),
                 tok_t(2 * ML_HEADS), tok(ML_WIDTH), tok_t(ML_WIDTH), tok(ML_WIDTH), tok(ML_WIDTH))
    return pl.pallas_call(
        _in_proj_kernel,
        out_shape=out_shape,
        grid=(b, nt),
        in_specs=[tok(d), pl.BlockSpec((1, 6, d), lambda i, j: (i, 0, 0)), full(nw), full(win), full(qln), full(wq),
                  full(cw), full(cb), full(gb), tok(LANE), tok(LANE)],
        out_specs=out_specs,
        scratch_shapes=[pltpu.VMEM((tm + 8, 2 * ML_WIDTH), f32)],
        compiler_params=pltpu.CompilerParams(dimension_semantics=("arbitrary", "arbitrary"),
                                             vmem_limit_bytes=VMEM_LIMIT),
        name="in_proj",
    )(x, mod_l, nw, win, qln, wq, cw, cb, gb, cos, sin)


def _key_to_f32(key):
    bits = jnp.where(key >= 0, key, key ^ jnp.int32(0x7FFFFFFF))
    return lax.bitcast_convert_type(bits, f32)


def _dsa_kernel(q_ref, qi_ref, kt_ref, v_ref, kit_ref, w_ref, nw_ref, o_ref,
                sc_ref, m_ref, l_ref, acc_ref, *, n_sel):
    i = pl.program_id(1)
    s_len = kt_ref.shape[2]
    sup = 4 * QB
    n_sup = i // 4 + 1
    t_col = i * QB + lax.broadcasted_iota(jnp.int32, (QB, 1), 0)

    qi = qi_ref[0]
    w = w_ref[0]
    qi_h = [qi[:, h * HEAD_DIM:(h + 1) * HEAD_DIM] for h in range(IDX_HEADS)]
    w_h = [w[:, MISC_W + h:MISC_W + h + 1] for h in range(IDX_HEADS)]
    lane_q = lax.broadcasted_iota(jnp.int32, (QB, QB), 1)

    def score_body(c, carry):
        for p in range(sup // QB):
            off = pl.multiple_of(c * sup + p * QB, QB)
            kc = kit_ref[0, :, pl.ds(off, QB)]
            acc = jnp.zeros((QB, QB), f32)
            for h in range(IDX_HEADS):
                acc = acc + w_h[h] * jnp.maximum(_dot(qi_h[h], kc), 0.0)
            sc_ref[:, pl.ds(off, QB)] = jnp.where(off + lane_q <= t_col, acc, -jnp.inf)
        return carry

    lax.fori_loop(0, n_sup, score_body, 0)

    lane_s = lax.broadcasted_iota(jnp.int32, (QB, sup), 1)

    def count(pred):
        def body(c, acc):
            off = pl.multiple_of(c * sup, sup)
            hit = jnp.where(pred(sc_ref[:, pl.ds(off, sup)], off + lane_s), 1.0, 0.0)
            for p in range(sup // LANE):
                acc = acc + hit[:, p * LANE:(p + 1) * LANE]
            return acc
        acc = lax.fori_loop(0, n_sup, body, jnp.zeros((QB, LANE), f32))
        return jnp.sum(acc, axis=1, keepdims=True)

    kf = float(n_sel)

    def select(_):
        def bisect(_, lh):
            lo, hi = lh
            mid = (lo & hi) + ((lo ^ hi) >> 1)
            thr = _key_to_f32(mid)
            ge = count(lambda blk, pos: blk >= thr) >= kf
            return jnp.where(ge, mid, lo), jnp.where(ge, hi, mid)

        lo, _ = lax.fori_loop(0, 32, bisect, (jnp.full((QB, 1), KEY_NEG_INF, jnp.int32),
                                              jnp.full((QB, 1), KEY_POS_INF, jnp.int32)))
        thr = _key_to_f32(lo)
        finite = thr > -jnp.inf
        n_ge = count(lambda blk, pos: blk >= thr)
        tied = jnp.logical_and(n_ge != kf, finite)
        cut0 = jnp.where(finite, s_len, -1)

        def break_ties(_):
            need = kf - count(lambda blk, pos: blk > thr)

            def bisect_idx(_, lh):
                lo_j, hi_j = lh
                mid = (lo_j + hi_j) >> 1
                ok = count(lambda blk, pos: jnp.logical_and(blk == thr, pos <= mid)) >= need
                return jnp.where(ok, lo_j, mid), jnp.where(ok, mid, hi_j)

            n_it = int(np.ceil(np.log2(s_len + 1)))
            _, hi_j = lax.fori_loop(0, n_it, bisect_idx, (jnp.full((QB, 1), -1, jnp.int32),
                                                          jnp.full((QB, 1), s_len - 1, jnp.int32)))
            return jnp.where(tied, hi_j, cut0)

        cut = lax.cond(jnp.max(tied.astype(jnp.int32)) > 0, break_ties, lambda _: cut0, 0)
        return thr, cut

    def select_all(_):
        return jnp.full((QB, 1), -jnp.inf, f32), jnp.full((QB, 1), -1, jnp.int32)

    thr, cut = lax.cond((i + 1) * QB > n_sel, select, select_all, 0)

    q = q_ref[0]
    q_h = [q[:, h * HEAD_DIM:(h + 1) * HEAD_DIM] for h in range(ATT_HEADS)]
    kc_len = 2 * QB
    lane_k = lax.broadcasted_iota(jnp.int32, (QB, kc_len), 1)
    m_ref[...] = jnp.full(m_ref.shape, NEG_BIG, f32)
    l_ref[...] = jnp.zeros(l_ref.shape, f32)
    acc_ref[...] = jnp.zeros(acc_ref.shape, f32)

    def att_body(c, carry):
        off = pl.multiple_of(c * kc_len, kc_len)
        blk = sc_ref[:, pl.ds(off, kc_len)]
        sel = jnp.logical_or(blk > thr, jnp.logical_and(blk == thr, off + lane_k <= cut))
        for h in range(ATT_HEADS):
            kt_h = kt_ref[0, h * HEAD_DIM:(h + 1) * HEAD_DIM, pl.ds(off, kc_len)]
            s = jnp.where(sel, _dot(q_h[h], kt_h), NEG_BIG)
            m_old = m_ref[h]
            m_new = jnp.maximum(m_old, jnp.max(s, axis=1, keepdims=True))
            alpha = jnp.exp(m_old - m_new)
            p = jnp.exp(s - m_new)
            l_ref[h] = alpha * l_ref[h] + jnp.sum(p, axis=1, keepdims=True)
            v_h = v_ref[0, pl.ds(off, kc_len), h * HEAD_DIM:(h + 1) * HEAD_DIM]
            acc_ref[h] = alpha * acc_ref[h] + _dot(p.astype(bf16), v_h)
            m_ref[h] = m_new
        return carry

    lax.fori_loop(0, (i + 2) // 2, att_body, 0)

    outs = []
    for h in range(ATT_HEADS):
        o = acc_ref[h] / l_ref[h]
        outs.append(_rms(o, nw_ref[:, h * HEAD_DIM:(h + 1) * HEAD_DIM]))
    o_ref[0] = jnp.concatenate(outs, axis=1).astype(bf16)


def _dsa(q, qi, kt, v, kit, misc, nw):
    b, s, _ = q.shape
    n_sel = min(TOPK, s // 4)
    s_pad = -(-s // (4 * QB)) * (4 * QB)
    kernel = functools.partial(_dsa_kernel, n_sel=n_sel)
    return pl.pallas_call(
        kernel,
        out_shape=jax.ShapeDtypeStruct((b, s, ATT_WIDTH), bf16),
        grid=(b, s // QB),
        in_specs=[pl.BlockSpec((1, QB, ATT_WIDTH), lambda i, j: (i, j, 0)),
                  pl.BlockSpec((1, QB, ATT_WIDTH), lambda i, j: (i, j, 0)),
                  pl.BlockSpec((1, ATT_WIDTH, s), lambda i, j: (i, 0, 0)),
                  pl.BlockSpec((1, s, ATT_WIDTH), lambda i, j: (i, 0, 0)),
                  pl.BlockSpec((1, HEAD_DIM, s), lambda i, j: (i, 0, 0)),
                  pl.BlockSpec((1, QB, LANE), lambda i, j: (i, j, 0)),
                  pl.BlockSpec((1, ATT_WIDTH), lambda i, j: (0, 0))],
        out_specs=pl.BlockSpec((1, QB, ATT_WIDTH), lambda i, j: (i, j, 0)),
        scratch_shapes=[pltpu.VMEM((QB, s_pad), f32),
                        pltpu.VMEM((ATT_HEADS, QB, 1), f32),
                        pltpu.VMEM((ATT_HEADS, QB, 1), f32),
                        pltpu.VMEM((ATT_HEADS, QB, HEAD_DIM), f32)],
        compiler_params=pltpu.CompilerParams(dimension_semantics=("arbitrary", "arbitrary"),
                                             vmem_limit_bytes=VMEM_LIMIT),
        name="dsa_attention",
    )(q, qi, kt, v, kit, misc, nw)


def _mlstm_kernel(qm_ref, kmt_ref, vm_ref, om_ref, misc_ref, gt_ref, nw_ref, hm_ref, c_ref, *, chunk):
    s_len = qm_ref.shape[1]
    n_chunks = s_len // chunk
    row = lax.broadcasted_iota(jnp.int32, (chunk, chunk), 0)
    col = lax.broadcasted_iota(jnp.int32, (chunk, chunk), 1)
    causal = col <= row
    tri = jnp.where(causal, 1.0, 0.0).astype(f32)
    tri_t = jnp.where(row <= col, 1.0, 0.0).astype(f32)
    lane = lax.broadcasted_iota(jnp.int32, (chunk, ML_DIM), 1)
    c_ref[...] = jnp.zeros(c_ref.shape, f32)

    def body(c, m_state):
        off = pl.multiple_of(c * chunk, chunk)
        g_col = misc_ref[0, pl.ds(off, chunk), :]
        g_row = gt_ref[0, :, pl.ds(off, chunk)]
        cum_col = _dot(tri, g_col, precision=lax.Precision.HIGHEST)
        cum_row = _dot(g_row, tri_t, precision=lax.Precision.HIGHEST)
        m_next = []
        for h in range(ML_HEADS):
            ig_col = g_col[:, MISC_I + h:MISC_I + h + 1]
            b_col = cum_col[:, MISC_F + h:MISC_F + h + 1]
            ig_row = g_row[h:h + 1, :]
            b_row = cum_row[ML_HEADS + h:ML_HEADS + h + 1, :]
            m_s = m_state[h]
            total = b_col[chunk - 1:chunk, :]
            hs = slice(h * ML_DIM, (h + 1) * ML_DIM)
            qh = qm_ref[0, pl.ds(off, chunk), hs]
            kth = kmt_ref[0, hs, pl.ds(off, chunk)]
            vh = vm_ref[0, pl.ds(off, chunk), hs]

            d_mat = jnp.where(causal, b_col - b_row + ig_row, -jnp.inf)
            inter = b_col + m_s
            m_row = jnp.maximum(inter, jnp.max(d_mat, axis=1, keepdims=True))
            w_intra = jnp.exp(d_mat - m_row)
            w_inter = jnp.exp(inter - m_row)
            s_mat = _dot(qh, kth) * w_intra
            c_ext = c_ref[h]
            qc = _dot(qh, c_ext.astype(bf16))
            num = _dot(s_mat.astype(bf16), vh) + w_inter * qc[:, :ML_DIM]
            den = jnp.sum(s_mat, axis=1, keepdims=True) + w_inter * qc[:, ML_DIM:ML_DIM + 1]
            hh = num / jnp.maximum(jnp.abs(den), jnp.exp(-m_row))

            w_log = total - b_col + ig_col
            m_new = jnp.maximum(total + m_s, jnp.max(w_log, axis=0, keepdims=True))
            decay = jnp.exp(total + m_s - m_new)
            w_state = jnp.exp(w_log - m_new)
            vw = jnp.concatenate([(w_state * vh.astype(f32)).astype(bf16),
                                  jnp.where(lane == 0, w_state, 0.0).astype(bf16)], axis=1)
            c_ref[h] = decay * c_ext + _dot(kth, vw)
            m_next.append(m_new)

            gate = jax.nn.sigmoid(om_ref[0, pl.ds(off, chunk), hs])
            hm_ref[0, pl.ds(off, chunk), hs] = (gate * _rms(hh, nw_ref[:, hs])).astype(bf16)
        return tuple(m_next)

    lax.fori_loop(0, n_chunks, body, tuple(jnp.zeros((1, 1), f32) for _ in range(ML_HEADS)))


def _mlstm(qm, kmt, vm, om, misc, gt, nw):
    b, s, _ = qm.shape
    chunk = min(ML_CHUNK, s)
    kernel = functools.partial(_mlstm_kernel, chunk=chunk)
    return pl.pallas_call(
        kernel,
        out_shape=jax.ShapeDtypeStruct((b, s, ML_WIDTH), bf16),
        grid=(b,),
        in_specs=[pl.BlockSpec((1, s, ML_WIDTH), lambda i: (i, 0, 0)),
                  pl.BlockSpec((1, ML_WIDTH, s), lambda i: (i, 0, 0)),
                  pl.BlockSpec((1, s, ML_WIDTH), lambda i: (i, 0, 0)),
                  pl.BlockSpec((1, s, ML_WIDTH), lambda i: (i, 0, 0)),
                  pl.BlockSpec((1, s, LANE), lambda i: (i, 0, 0)),
                  pl.BlockSpec((1, 2 * ML_HEADS, s), lambda i: (i, 0, 0)),
                  pl.BlockSpec((1, ML_WIDTH), lambda i: (0, 0))],
        out_specs=pl.BlockSpec((1, s, ML_WIDTH), lambda i: (i, 0, 0)),
        scratch_shapes=[pltpu.VMEM((ML_HEADS, ML_DIM, 2 * ML_DIM), f32)],
        compiler_params=pltpu.CompilerParams(dimension_semantics=("arbitrary",),
                                             vmem_limit_bytes=VMEM_LIMIT),
        name="mlstm",
    )(qm, kmt, vm, om, misc, gt, nw)


def _out_ffn_kernel(att_ref, hm_ref, x_ref, mod_ref, wout_ref, npost_ref, fpre_ref, wgu_ref, wdn_ref, fpost_ref,
                    o_ref, *, n_split):
    hid = wdn_ref.shape[0]
    g_m = mod_ref[0, 2:3, :]
    sh_f = mod_ref[0, 3:4, :]
    sc_f = mod_ref[0, 4:5, :]
    g_f = mod_ref[0, 5:6, :]
    y = _dot(att_ref[0], wout_ref[0:ATT_WIDTH, :]) + _dot(hm_ref[0], wout_ref[ATT_WIDTH:, :])
    x1 = x_ref[0] + g_m * _rms(y, npost_ref[...])
    h2 = (_rms(x1, fpre_ref[...]) * (1.0 + sc_f) + sh_f).astype(bf16)
    step = hid // n_split
    y2 = jnp.zeros(x1.shape, f32)
    for c in range(n_split):
        g = _dot(h2, wgu_ref[:, c * step:(c + 1) * step])
        u = _dot(h2, wgu_ref[:, hid + c * step:hid + (c + 1) * step])
        a = (g * jax.nn.sigmoid(g) * u).astype(bf16)
        y2 = y2 + _dot(a, wdn_ref[c * step:(c + 1) * step, :])
    o_ref[0] = x1 + g_f * _rms(y2, fpost_ref[...])


def _out_ffn(att, hm, x, mod_l, wout, npost, fpre, wgu, wdn, fpost, tm):
    b, s, d = x.shape
    hid = wdn.shape[0]
    n_split = 2 if hid % (2 * LANE) == 0 else 1

    def tok(w):
        return pl.BlockSpec((1, tm, w), lambda i, j: (i, j, 0))

    def const(a):
        return pl.BlockSpec(a.shape, lambda i, j: (0,) * a.ndim, pipeline_mode=pl.Buffered(1))

    kernel = functools.partial(_out_ffn_kernel, n_split=n_split)
    return pl.pallas_call(
        kernel,
        out_shape=jax.ShapeDtypeStruct((b, s, d), f32),
        grid=(b, s // tm),
        in_specs=[tok(ATT_WIDTH), tok(ML_WIDTH), tok(d), pl.BlockSpec((1, 6, d), lambda i, j: (i, 0, 0)),
                  const(wout), const(npost), const(fpre), const(wgu), const(wdn), const(fpost)],
        out_specs=tok(d),
        compiler_params=pltpu.CompilerParams(dimension_semantics=("arbitrary", "arbitrary"),
                                             vmem_limit_bytes=VMEM_LIMIT),
        name="out_ffn",
    )(att, hm, x, mod_l, wout, npost, fpre, wgu, wdn, fpost)


def _arrange_w_in(w):
    d = w.shape[0]
    pts = [int(p) for p in np.cumsum(IN_SPLITS)[:-1]]
    cq, k_att, v_att, k_idx, w_idx, q_m, k_m, v_m, o_m, i_m, f_m = jnp.split(w, pts, axis=1)
    misc = jnp.concatenate([k_idx, w_idx, i_m, f_m, jnp.zeros((d, LANE - MISC_END), w.dtype)], axis=1)
    return jnp.concatenate([cq, k_att, v_att, misc, q_m, k_m, v_m, o_m], axis=1).astype(bf16)


def kernel(x, c, positions, w_mod, b_mod, mix_norm_pre, mix_norm_post, w_in, q_latent_norm, w_q_up, w_qidx_up,
           conv_w, conv_b, b_igate, b_fgate, attn_out_norm, mlstm_out_norm, w_out, ffn_norm_pre, ffn_norm_post,
           w_gate_up, w_down):
    depth = w_mod.shape[0]
    b, s, d = x.shape
    tm = min(512, s)
    mod = _modulation(c, w_mod, b_mod).reshape(depth, b, 6, d)
    cos, sin = _rope_tables(positions)

    def row(a):
        return a.reshape(1, -1)

    for l in range(depth):
        win = _arrange_w_in(w_in[l])
        wq = jnp.concatenate([w_q_up[l], w_qidx_up[l]], axis=1).astype(bf16)
        gb = jnp.zeros((1, LANE), f32).at[0, MISC_I:MISC_F].set(b_igate[l]).at[0, MISC_F:MISC_END].set(b_fgate[l])
        q, qi, kt, v, kit, misc, gt, qm, kmt, vm, om = _in_proj(
            x, mod[l], row(mix_norm_pre[l]), win, row(q_latent_norm[l]), wq, conv_w[l], row(conv_b[l]), gb,
            cos, sin, tm)
        att = _dsa(q, qi, kt, v, kit, misc, row(attn_out_norm[l]))
        hm = _mlstm(qm, kmt, vm, om, misc, gt, row(mlstm_out_norm[l]))
        x = _out_ffn(att, hm, x, mod[l], w_out[l].astype(bf16), row(mix_norm_post[l]), row(ffn_norm_pre[l]),
                     w_gate_up[l].astype(bf16), w_down[l].astype(bf16), row(ffn_norm_post[l]), tm)
    return x
```

```python
import functools

import numpy as np
import jax
import jax.numpy as jnp
from jax import lax
from jax.experimental import pallas as pl
from jax.experimental.pallas import tpu as pltpu

EPS = 1e-6
ROPE_THETA = 10000.0
ATT_HEADS = 8
HEAD_DIM = 64
ATT_WIDTH = ATT_HEADS * HEAD_DIM
Q_RANK = 256
IDX_HEADS = 8
ML_HEADS = 4
ML_DIM = 128
ML_WIDTH = ML_HEADS * ML_DIM
CONV_W = 4
TOPK = 256
QB = 128
LANE = 128
ML_CHUNK = 256
IN_SPLITS = (Q_RANK, ATT_WIDTH, ATT_WIDTH, HEAD_DIM, IDX_HEADS,
             ML_WIDTH, ML_WIDTH, ML_WIDTH, ML_WIDTH, ML_HEADS, ML_HEADS)

OFF_CQ = 0
OFF_K = OFF_CQ + Q_RANK
OFF_V = OFF_K + ATT_WIDTH
OFF_MISC = OFF_V + ATT_WIDTH
OFF_QKM = OFF_MISC + LANE
OFF_VM = OFF_QKM + 2 * ML_WIDTH
OFF_OM = OFF_VM + ML_WIDTH
IN_COLS_PADDED = OFF_OM + ML_WIDTH
MISC_W = HEAD_DIM
MISC_I = MISC_W + IDX_HEADS
MISC_F = MISC_I + ML_HEADS
MISC_END = MISC_F + ML_HEADS

VMEM_LIMIT = 56 * 1024 * 1024
NEG_BIG = -1e30
KEY_POS_INF = 0x7F800000
KEY_NEG_INF = (0xFF800000 - (1 << 32)) ^ 0x7FFFFFFF

bf16 = jnp.bfloat16
f32 = jnp.float32


def _rms(x, w):
    return x * lax.rsqrt(jnp.mean(x * x, axis=-1, keepdims=True) + EPS) * w


def _dot(a, b, precision=None):
    return jnp.dot(a, b, preferred_element_type=f32, precision=precision)


def _rope(x, cos, sin_signed):
    n = x.shape[1]
    lane = lax.broadcasted_iota(jnp.int32, x.shape, 1)
    low = (lane & (HEAD_DIM // 2)) == 0
    rot = jnp.where(low, pltpu.roll(x, n - HEAD_DIM // 2, 1), pltpu.roll(x, HEAD_DIM // 2, 1))
    return x * cos + rot * sin_signed


def _mod_kernel(c_ref, w_ref, b_ref, o_ref):
    c = c_ref[...]
    c_act = c * jax.nn.sigmoid(c)
    o_ref[0] = _dot(c_act, w_ref[0], precision=lax.Precision.HIGHEST) + b_ref[0]


def _modulation(c, w_mod, b_mod):
    depth, d, d6 = w_mod.shape
    b = c.shape[0]
    nj = d6 // d
    return pl.pallas_call(
        _mod_kernel,
        out_shape=jax.ShapeDtypeStruct((depth, b, d6), f32),
        grid=(depth, nj),
        in_specs=[pl.BlockSpec((b, d), lambda l, j: (0, 0)),
                  pl.BlockSpec((1, d, d), lambda l, j: (l, 0, j)),
                  pl.BlockSpec((1, 1, d), lambda l, j: (l, 0, j))],
        out_specs=pl.BlockSpec((1, b, d), lambda l, j: (l, 0, j)),
        name="modulation",
    )(c, w_mod, b_mod.reshape(depth, 1, d6))


def _rope_table_kernel(pos_ref, invf_ref, sign_ref, cos_ref, sin_ref):
    ang = pos_ref[0].astype(f32) * invf_ref[...]
    cos_ref[0] = jnp.cos(ang)
    sin_ref[0] = jnp.sin(ang) * sign_ref[...]


def _rope_tables(positions):
    b, s = positions.shape
    ts = min(s, 512)
    inv_freq = ROPE_THETA ** (-jnp.arange(0, HEAD_DIM, 2, dtype=f32) / HEAD_DIM)
    invf = jnp.tile(inv_freq, LANE // (HEAD_DIM // 2)).reshape(1, LANE)
    sign = np.where((np.arange(LANE) % HEAD_DIM) < HEAD_DIM // 2, -1.0, 1.0).astype(np.float32).reshape(1, LANE)
    tab = jax.ShapeDtypeStruct((b, s, LANE), f32)
    return pl.pallas_call(
        _rope_table_kernel,
        out_shape=(tab, tab),
        grid=(b, s // ts),
        in_specs=[pl.BlockSpec((1, ts, 1), lambda i, j: (i, j, 0)),
                  pl.BlockSpec((1, LANE), lambda i, j: (0, 0)),
                  pl.BlockSpec((1, LANE), lambda i, j: (0, 0))],
        out_specs=(pl.BlockSpec((1, ts, LANE), lambda i, j: (i, j, 0)),
                   pl.BlockSpec((1, ts, LANE), lambda i, j: (i, j, 0))),
        name="rope_tables",
    )(positions.reshape(b, s, 1), invf, jnp.asarray(sign))


def _in_proj_kernel(x_ref, mod_ref, nw_ref, win_ref, qln_ref, wq_ref, cw_ref, cb_ref, gb_ref, cos_ref, sin_ref,
                    qt_ref, qit_ref, k_ref, vt_ref, kidx_ref, wt_ref, misc_ref, gt_ref, qm_ref, kmt_ref, vm_ref,
                    om_ref, buf_ref):
    j = pl.program_id(1)
    tm = x_ref.shape[1]
    x = x_ref[0]
    sh = mod_ref[0, 0:1, :]
    sc = mod_ref[0, 1:2, :]
    hb = (_rms(x, nw_ref[...]) * (1.0 + sc) + sh).astype(bf16)

    cos = cos_ref[0]
    sin = sin_ref[0]
    reps = ATT_WIDTH // LANE
    cos_w = jnp.concatenate([cos] * reps, axis=1)
    sin_w = jnp.concatenate([sin] * reps, axis=1)

    cq = _dot(hb, win_ref[:, OFF_CQ:OFF_CQ + Q_RANK])
    cqn = _rms(cq, qln_ref[...]).astype(bf16)
    qq = _dot(cqn, wq_ref[...])
    q_t = (_rope(qq[:, :ATT_WIDTH], cos_w, sin_w) * (HEAD_DIM ** -0.5)).T.astype(bf16)
    qi_t = _rope(qq[:, ATT_WIDTH:], cos_w, sin_w).T.astype(bf16)
    qt_ref[...] = jnp.zeros(qt_ref.shape, bf16)
    for rb in range(tm // QB):
        rows = slice(rb * QB, (rb + 1) * QB)
        for h in range(ATT_HEADS):
            d0 = (h % 2) * HEAD_DIM
            qt_ref[0, rb, h // 2, d0:d0 + HEAD_DIM, (h % 2) * QB:(h % 2 + 1) * QB] = (
                q_t[h * HEAD_DIM:(h + 1) * HEAD_DIM, rows])
        for h in range(IDX_HEADS):
            qit_ref[0, rb, :, h * QB:(h + 1) * QB] = qi_t[h * HEAD_DIM:(h + 1) * HEAD_DIM, rows]
    k_ref[0] = _rope(_dot(hb, win_ref[:, OFF_K:OFF_K + ATT_WIDTH]), cos_w, sin_w).astype(bf16)
    vt_ref[0] = _dot(hb, win_ref[:, OFF_V:OFF_V + ATT_WIDTH]).T.astype(bf16)

    m = _dot(hb, win_ref[:, OFF_MISC:OFF_MISC + LANE])
    lane = lax.broadcasted_iota(jnp.int32, m.shape, 1)
    roped = _rope(m, cos, sin)
    biased = m + gb_ref[...]
    logsig = jnp.minimum(biased, 0.0) - jnp.log1p(jnp.exp(-jnp.abs(biased)))
    w_scale = (IDX_HEADS * HEAD_DIM) ** -0.5
    misc = jnp.where(lane < MISC_W, roped,
                     jnp.where(lane < MISC_I, m * w_scale,
                               jnp.where(lane < MISC_F, biased,
                                         jnp.where(lane < MISC_END, logsig, 0.0))))
    misc_ref[0] = misc
    misc_t = misc.T
    kidx_ref[0] = misc[:, 0:HEAD_DIM].astype(bf16)
    wt_ref[0] = misc_t[MISC_W:MISC_I, :]
    gt_ref[0] = misc_t[MISC_I:MISC_END, :]

    pre = _dot(hb, win_ref[:, OFF_QKM:OFF_QKM + 2 * ML_WIDTH])

    @pl.when(j == 0)
    def _():
        buf_ref[0:8, :] = jnp.zeros((8, 2 * ML_WIDTH), f32)

    buf_ref[8:8 + tm, :] = pre
    acc = jnp.broadcast_to(cb_ref[...], pre.shape)
    for t in range(CONV_W):
        acc = acc + cw_ref[t:t + 1, :] * buf_ref[8 - (CONV_W - 1) + t:8 - (CONV_W - 1) + t + tm, :]
    buf_ref[0:8, :] = buf_ref[tm:tm + 8, :]
    qk = acc * jax.nn.sigmoid(acc)
    qm_ref[0] = (qk[:, :ML_WIDTH] * (ML_DIM ** -0.5)).astype(bf16)
    kmt_ref[0] = qk[:, ML_WIDTH:].T.astype(bf16)
    vm_ref[0] = _dot(hb, win_ref[:, OFF_VM:OFF_VM + ML_WIDTH]).astype(bf16)
    om_ref[0] = _dot(hb, win_ref[:, OFF_OM:OFF_OM + ML_WIDTH])


def _in_proj(x, mod_l, nw, win, qln, wq, cw, cb, gb, cos, sin, tm):
    b, s, d = x.shape
    nt = s // tm

    def tok(w):
        return pl.BlockSpec((1, tm, w), lambda i, j: (i, j, 0))

    def tok_t(r):
        return pl.BlockSpec((1, r, tm), lambda i, j: (i, 0, j))

    def full(a):
        return pl.BlockSpec(a.shape, lambda i, j: (0,) * a.ndim)

    sd = jax.ShapeDtypeStruct
    n_rb, rb_t = s // QB, tm // QB
    qt_shape = (ATT_HEADS // 2, 2 * HEAD_DIM, 2 * QB)
    qit_shape = (HEAD_DIM, IDX_HEADS * QB)
    out_shape = (sd((b, n_rb) + qt_shape, bf16), sd((b, n_rb) + qit_shape, bf16), sd((b, s, ATT_WIDTH), bf16),
                 sd((b, ATT_WIDTH, s), bf16), sd((b, s, HEAD_DIM), bf16), sd((b, IDX_HEADS, s), f32),
                 sd((b, s, LANE), f32), sd((b, 2 * ML_HEADS, s), f32), sd((b, s, ML_WIDTH), bf16),
                 sd((b, ML_WIDTH, s), bf16), sd((b, s, ML_WIDTH), bf16), sd((b, s, ML_WIDTH), f32))
    out_specs = (pl.BlockSpec((1, rb_t) + qt_shape, lambda i, j: (i, j, 0, 0, 0)),
                 pl.BlockSpec((1, rb_t) + qit_shape, lambda i, j: (i, j, 0, 0)),
                 tok(ATT_WIDTH), tok_t(ATT_WIDTH), tok(HEAD_DIM), tok_t(IDX_HEADS), tok(LANE),
                 tok_t(2 * ML_HEADS), tok(ML_WIDTH), tok_t(ML_WIDTH), tok(ML_WIDTH), tok(ML_WIDTH))
    return pl.pallas_call(
        _in_proj_kernel,
        out_shape=out_shape,
        grid=(b, nt),
        in_specs=[tok(d), pl.BlockSpec((1, 6, d), lambda i, j: (i, 0, 0)), full(nw), full(win), full(qln), full(wq),
                  full(cw), full(cb), full(gb), tok(LANE), tok(LANE)],
        out_specs=out_specs,
        scratch_shapes=[pltpu.VMEM((tm + 8, 2 * ML_WIDTH), f32)],
        compiler_params=pltpu.CompilerParams(dimension_semantics=("arbitrary", "arbitrary"),
                                             vmem_limit_bytes=VMEM_LIMIT),
        name="in_proj",
    )(x, mod_l, nw, win, qln, wq, cw, cb, gb, cos, sin)


def _key_to_f32(key):
    bits = jnp.where(key >= 0, key, key ^ jnp.int32(0x7FFFFFFF))
    return lax.bitcast_convert_type(bits, f32)


def _fold8(x, op):
    parts = [x[i * 8:(i + 1) * 8] for i in range(x.shape[0] // 8)]
    while len(parts) > 1:
        parts = [op(parts[i], parts[i + 1]) for i in range(0, len(parts), 2)]
    return parts[0]


def _dsa_kernel(qit_ref, kidx_ref, wt_ref, qt_ref, k_ref, vt_ref, nw_ref, o_ref, sc_ref, lo_ref, hi_ref, *, n_sel):
    s_len = k_ref.shape[1]
    n_rb = s_len // QB
    sup = 4 * QB
    kf = float(n_sel)
    key_i = lax.broadcasted_iota(jnp.int32, (QB, QB), 0)
    row_i = lax.broadcasted_iota(jnp.int32, (QB, QB), 1)

    def col_sum(x):
        return jnp.sum(x, axis=0, keepdims=True)

    def score_rows(r, carry):
        roff = pl.multiple_of(r * QB, QB)
        qit = qit_ref[0, r]
        w = wt_ref[0, :, pl.ds(roff, QB)]

        def score_keys(c, carry2):
            for p in range(sup // QB):
                off = pl.multiple_of(c * sup + p * QB, QB)
                lg = _dot(kidx_ref[0, pl.ds(off, QB), :], qit)
                acc = jnp.zeros((QB, QB), f32)
                for h in range(IDX_HEADS):
                    acc = acc + w[h:h + 1, :] * jnp.maximum(lg[:, h * QB:(h + 1) * QB], 0.0)
                sc_ref[pl.ds(off, QB), pl.ds(roff, QB)] = jnp.where(off + key_i <= roff + row_i, acc, -jnp.inf)
            return carry2

        lax.fori_loop(0, r // 4 + 1, score_keys, 0)
        return carry

    lax.fori_loop(0, n_rb, score_rows, 0)

    lo_ref[...] = jnp.full(lo_ref.shape, KEY_NEG_INF, jnp.int32)
    hi_ref[...] = jnp.full(hi_ref.shape, KEY_POS_INF, jnp.int32)
    first = n_sel // QB

    def bisect(_, carry):
        for r in range(first, n_rb):
            cols = slice(r * QB, (r + 1) * QB)
            lo = lo_ref[r, 0:1, :]
            hi = hi_ref[r, 0:1, :]
            mid = (lo & hi) + ((lo ^ hi) >> 1)
            thr = _key_to_f32(mid)
            part = jnp.zeros((QB, QB), f32)
            for c in range(r + 1):
                part = part + jnp.where(sc_ref[c * QB:(c + 1) * QB, cols] >= thr, 1.0, 0.0)
            ge = col_sum(part) >= kf
            lo_ref[r, 0:1, :] = jnp.where(ge, mid, lo)
            hi_ref[r, 0:1, :] = jnp.where(ge, hi, mid)
        return carry

    lax.fori_loop(0, 32, bisect, 0)

    def finalize(r, carry):
        roff = pl.multiple_of(r * QB, QB)
        thr = _key_to_f32(lo_ref[r, 0:1, :])

        def count(pred):
            def body(c, part):
                off = pl.multiple_of(c * QB, QB)
                blk = sc_ref[pl.ds(off, QB), pl.ds(roff, QB)]
                return part + jnp.where(pred(blk, off + key_i), 1.0, 0.0)
            return col_sum(lax.fori_loop(0, r + 1, body, jnp.zeros((QB, QB), f32)))

        finite = thr > -jnp.inf
        tied = jnp.logical_and(count(lambda blk, pos: blk >= thr) != kf, finite)
        cut0 = jnp.where(finite, s_len, -1)

        def break_ties(_):
            need = kf - count(lambda blk, pos: blk > thr)

            def bisect_idx(_, lh):
                lo_j, hi_j = lh
                mid = (lo_j + hi_j) >> 1
                ok = count(lambda blk, pos: jnp.logical_and(blk == thr, pos <= mid)) >= need
                return jnp.where(ok, lo_j, mid), jnp.where(ok, mid, hi_j)

            n_it = int(np.ceil(np.log2(s_len + 1)))
            _, hi_j = lax.fori_loop(0, n_it, bisect_idx, (jnp.full((1, QB), -1, jnp.int32),
                                                          jnp.full((1, QB), s_len - 1, jnp.int32)))
            return jnp.where(tied, hi_j, cut0)

        cut = lax.cond(jnp.max(tied.astype(jnp.int32)) > 0, break_ties, lambda _: cut0, 0)

        def write_mask(c, carry2):
            off = pl.multiple_of(c * QB, QB)
            blk = sc_ref[pl.ds(off, QB), pl.ds(roff, QB)]
            sel = jnp.logical_or(blk > thr, jnp.logical_and(blk == thr, off + key_i <= cut))
            sc_ref[pl.ds(off, QB), pl.ds(roff, QB)] = jnp.where(sel, 0.0, NEG_BIG)
            return carry2

        lax.fori_loop(0, 4 * (r // 4 + 1), write_mask, 0)
        return carry

    lax.fori_loop(0, n_rb, finalize, 0)

    def attend(r, carry):
        roff = pl.multiple_of(r * QB, QB)
        n_sup = r // 4 + 1

        def logits(c):
            off = pl.multiple_of(c * sup, sup)
            bias = sc_ref[pl.ds(off, sup), pl.ds(roff, QB)]
            out = []
            for hp in range(ATT_HEADS // 2):
                k_pair = k_ref[0, pl.ds(off, sup), hp * 2 * HEAD_DIM:(hp + 1) * 2 * HEAD_DIM]
                s2 = _dot(k_pair, qt_ref[0, r, hp])
                out += [s2[:, :QB] + bias, s2[:, QB:] + bias]
            return out, off

        def key_step(c, carry2):
            m_run, l_run, acc = carry2
            s_all, off = logits(c)
            m_new, l_new, acc_new = [], [], []
            for h in range(ATT_HEADS):
                m_h = jnp.maximum(m_run[h], jnp.max(_fold8(s_all[h], jnp.maximum), axis=0, keepdims=True))
                alpha = jnp.exp(m_run[h] - m_h)
                p = jnp.exp(s_all[h] - m_h)
                l_new.append(alpha * l_run[h] + col_sum(_fold8(p, jnp.add)))
                vt_h = vt_ref[0, h * HEAD_DIM:(h + 1) * HEAD_DIM, pl.ds(off, sup)]
                acc_new.append(alpha * acc[h] + _dot(vt_h, p.astype(bf16)))
                m_new.append(m_h)
            return tuple(m_new), tuple(l_new), tuple(acc_new)

        _, l_run, acc = lax.fori_loop(0, n_sup, key_step,
                                      (tuple(jnp.full((1, QB), NEG_BIG, f32) for _ in range(ATT_HEADS)),
                                       tuple(jnp.zeros((1, QB), f32) for _ in range(ATT_HEADS)),
                                       tuple(jnp.zeros((HEAD_DIM, QB), f32) for _ in range(ATT_HEADS))))
        outs = []
        for h in range(ATT_HEADS):
            o = acc[h] / l_run[h]
            inv = lax.rsqrt(jnp.mean(o * o, axis=0, keepdims=True) + EPS)
            outs.append(o * inv * nw_ref[h * HEAD_DIM:(h + 1) * HEAD_DIM, :])
        o_ref[0, pl.ds(roff, QB), :] = jnp.concatenate(outs, axis=0).T.astype(bf16)
        return carry

    lax.fori_loop(0, n_rb, attend, 0)


def _dsa(qt, qit, k, vt, kidx, wt, nw):
    b, s, _ = k.shape
    n_rb = s // QB
    n_sel = min(TOPK, s // 4)
    s_pad = -(-s // (4 * QB)) * (4 * QB)
    kernel = functools.partial(_dsa_kernel, n_sel=n_sel)

    def per_batch(a):
        return pl.BlockSpec((1,) + a.shape[1:], lambda i: (i,) + (0,) * (a.ndim - 1))

    nw_col = nw.reshape(ATT_WIDTH, 1)
    return pl.pallas_call(
        kernel,
        out_shape=jax.ShapeDtypeStruct((b, s, ATT_WIDTH), bf16),
        grid=(b,),
        in_specs=[per_batch(qit), per_batch(kidx), per_batch(wt), per_batch(qt), per_batch(k), per_batch(vt),
                  pl.BlockSpec((ATT_WIDTH, 1), lambda i: (0, 0))],
        out_specs=pl.BlockSpec((1, s, ATT_WIDTH), lambda i: (i, 0, 0)),
        scratch_shapes=[pltpu.VMEM((s_pad, s), f32),
                        pltpu.VMEM((n_rb, 8, QB), jnp.int32),
                        pltpu.VMEM((n_rb, 8, QB), jnp.int32)],
        compiler_params=pltpu.CompilerParams(dimension_semantics=("arbitrary",),
                                             vmem_limit_bytes=VMEM_LIMIT),
        name="dsa_attention",
    )(qit, kidx, wt, qt, k, vt, nw_col)


def _mlstm_kernel(qm_ref, kmt_ref, vm_ref, om_ref, misc_ref, gt_ref, nw_ref, hm_ref, c_ref, *, chunk):
    s_len = qm_ref.shape[1]
    n_chunks = s_len // chunk
    row = lax.broadcasted_iota(jnp.int32, (chunk, chunk), 0)
    col = lax.broadcasted_iota(jnp.int32, (chunk, chunk), 1)
    causal = col <= row
    tri = jnp.where(causal, 1.0, 0.0).astype(f32)
    tri_t = jnp.where(row <= col, 1.0, 0.0).astype(f32)
    lane = lax.broadcasted_iota(jnp.int32, (chunk, ML_DIM), 1)
    c_ref[...] = jnp.zeros(c_ref.shape, f32)

    def body(c, m_state):
        off = pl.multiple_of(c * chunk, chunk)
        g_col = misc_ref[0, pl.ds(off, chunk), :]
        g_row = gt_ref[0, :, pl.ds(off, chunk)]
        cum_col = _dot(tri, g_col, precision=lax.Precision.HIGHEST)
        cum_row = _dot(g_row, tri_t, precision=lax.Precision.HIGHEST)
        m_next = []
        for h in range(ML_HEADS):
            ig_col = g_col[:, MISC_I + h:MISC_I + h + 1]
            b_col = cum_col[:, MISC_F + h:MISC_F + h + 1]
            ig_row = g_row[h:h + 1, :]
            b_row = cum_row[ML_HEADS + h:ML_HEADS + h + 1, :]
            m_s = m_state[h]
            total = b_col[chunk - 1:chunk, :]
            hs = slice(h * ML_DIM, (h + 1) * ML_DIM)
            qh = qm_ref[0, pl.ds(off, chunk), hs]
            kth = kmt_ref[0, hs, pl.ds(off, chunk)]
            vh = vm_ref[0, pl.ds(off, chunk), hs]

            d_mat = jnp.where(causal, b_col - b_row + ig_row, -jnp.inf)
            inter = b_col + m_s
            m_row = jnp.maximum(inter, jnp.max(d_mat, axis=1, keepdims=True))
            w_intra = jnp.exp(d_mat - m_row)
            w_inter = jnp.exp(inter - m_row)
            s_mat = _dot(qh, kth) * w_intra
            c_ext = c_ref[h]
            qc = _dot(qh, c_ext.astype(bf16))
            num = _dot(s_mat.astype(bf16), vh) + w_inter * qc[:, :ML_DIM]
            den = jnp.sum(s_mat, axis=1, keepdims=True) + w_inter * qc[:, ML_DIM:ML_DIM + 1]
            hh = num / jnp.maximum(jnp.abs(den), jnp.exp(-m_row))

            w_log = total - b_col + ig_col
            m_new = jnp.maximum(total + m_s, jnp.max(w_log, axis=0, keepdims=True))
            decay = jnp.exp(total + m_s - m_new)
            w_state = jnp.exp(w_log - m_new)
            vw = jnp.concatenate([(w_state * vh.astype(f32)).astype(bf16),
                                  jnp.where(lane == 0, w_state, 0.0).astype(bf16)], axis=1)
            c_ref[h] = decay * c_ext + _dot(kth, vw)
            m_next.append(m_new)

            gate = jax.nn.sigmoid(om_ref[0, pl.ds(off, chunk), hs])
            hm_ref[0, pl.ds(off, chunk), hs] = (gate * _rms(hh, nw_ref[:, hs])).astype(bf16)
        return tuple(m_next)

    lax.fori_loop(0, n_chunks, body, tuple(jnp.zeros((1, 1), f32) for _ in range(ML_HEADS)))


def _mlstm(qm, kmt, vm, om, misc, gt, nw):
    b, s, _ = qm.shape
    chunk = min(ML_CHUNK, s)
    kernel = functools.partial(_mlstm_kernel, chunk=chunk)
    return pl.pallas_call(
        kernel,
        out_shape=jax.ShapeDtypeStruct((b, s, ML_WIDTH), bf16),
        grid=(b,),
        in_specs=[pl.BlockSpec((1, s, ML_WIDTH), lambda i: (i, 0, 0)),
                  pl.BlockSpec((1, ML_WIDTH, s), lambda i: (i, 0, 0)),
                  pl.BlockSpec((1, s, ML_WIDTH), lambda i: (i, 0, 0)),
                  pl.BlockSpec((1, s, ML_WIDTH), lambda i: (i, 0, 0)),
                  pl.BlockSpec((1, s, LANE), lambda i: (i, 0, 0)),
                  pl.BlockSpec((1, 2 * ML_HEADS, s), lambda i: (i, 0, 0)),
                  pl.BlockSpec((1, ML_WIDTH), lambda i: (0, 0))],
        out_specs=pl.BlockSpec((1, s, ML_WIDTH), lambda i: (i, 0, 0)),
        scratch_shapes=[pltpu.VMEM((ML_HEADS, ML_DIM, 2 * ML_DIM), f32)],
        compiler_params=pltpu.CompilerParams(dimension_semantics=("arbitrary",),
                                             vmem_limit_bytes=VMEM_LIMIT),
        name="mlstm",
    )(qm, kmt, vm, om, misc, gt, nw)


def _out_ffn_kernel(att_ref, hm_ref, x_ref, mod_ref, wout_ref, npost_ref, fpre_ref, wgu_ref, wdn_ref, fpost_ref,
                    o_ref, *, n_split):
    hid = wdn_ref.shape[0]
    g_m = mod_ref[0, 2:3, :]
    sh_f = mod_ref[0, 3:4, :]
    sc_f = mod_ref[0, 4:5, :]
    g_f = mod_ref[0, 5:6, :]
    y = _dot(att_ref[0], wout_ref[0:ATT_WIDTH, :]) + _dot(hm_ref[0], wout_ref[ATT_WIDTH:, :])
    x1 = x_ref[0] + g_m * _rms(y, npost_ref[...])
    h2 = (_rms(x1, fpre_ref[...]) * (1.0 + sc_f) + sh_f).astype(bf16)
    step = hid // n_split
    y2 = jnp.zeros(x1.shape, f32)
    for c in range(n_split):
        g = _dot(h2, wgu_ref[:, c * step:(c + 1) * step])
        u = _dot(h2, wgu_ref[:, hid + c * step:hid + (c + 1) * step])
        a = (g * jax.nn.sigmoid(g) * u).astype(bf16)
        y2 = y2 + _dot(a, wdn_ref[c * step:(c + 1) * step, :])
    o_ref[0] = x1 + g_f * _rms(y2, fpost_ref[...])


def _out_ffn(att, hm, x, mod_l, wout, npost, fpre, wgu, wdn, fpost, tm):
    b, s, d = x.shape
    hid = wdn.shape[0]
    n_split = 2 if hid % (2 * LANE) == 0 else 1

    def tok(w):
        return pl.BlockSpec((1, tm, w), lambda i, j: (i, j, 0))

    def const(a):
        return pl.BlockSpec(a.shape, lambda i, j: (0,) * a.ndim, pipeline_mode=pl.Buffered(1))

    kernel = functools.partial(_out_ffn_kernel, n_split=n_split)
    return pl.pallas_call(
        kernel,
        out_shape=jax.ShapeDtypeStruct((b, s, d), f32),
        grid=(b, s // tm),
        in_specs=[tok(ATT_WIDTH), tok(ML_WIDTH), tok(d), pl.BlockSpec((1, 6, d), lambda i, j: (i, 0, 0)),
                  const(wout), const(npost), const(fpre), const(wgu), const(wdn), const(fpost)],
        out_specs=tok(d),
        compiler_params=pltpu.CompilerParams(dimension_semantics=("arbitrary", "arbitrary"),
                                             vmem_limit_bytes=VMEM_LIMIT),
        name="out_ffn",
    )(att, hm, x, mod_l, wout, npost, fpre, wgu, wdn, fpost)


def _arrange_w_in(w):
    d = w.shape[0]
    pts = [int(p) for p in np.cumsum(IN_SPLITS)[:-1]]
    cq, k_att, v_att, k_idx, w_idx, q_m, k_m, v_m, o_m, i_m, f_m = jnp.split(w, pts, axis=1)
    misc = jnp.concatenate([k_idx, w_idx, i_m, f_m, jnp.zeros((d, LANE - MISC_END), w.dtype)], axis=1)
    return jnp.concatenate([cq, k_att, v_att, misc, q_m, k_m, v_m, o_m], axis=1).astype(bf16)


def kernel(x, c, positions, w_mod, b_mod, mix_norm_pre, mix_norm_post, w_in, q_latent_norm, w_q_up, w_qidx_up,
           conv_w, conv_b, b_igate, b_fgate, attn_out_norm, mlstm_out_norm, w_out, ffn_norm_pre, ffn_norm_post,
           w_gate_up, w_down):
    depth = w_mod.shape[0]
    b, s, d = x.shape
    tm = min(512, s)
    mod = _modulation(c, w_mod, b_mod).reshape(depth, b, 6, d)
    cos, sin = _rope_tables(positions)

    def row(a):
        return a.reshape(1, -1)

    for l in range(depth):
        win = _arrange_w_in(w_in[l])
        wq = jnp.concatenate([w_q_up[l], w_qidx_up[l]], axis=1).astype(bf16)
        gb = jnp.zeros((1, LANE), f32).at[0, MISC_I:MISC_F].set(b_igate[l]).at[0, MISC_F:MISC_END].set(b_fgate[l])
        qt, qit, k, vt, kidx, wt, misc, gt, qm, kmt, vm, om = _in_proj(
            x, mod[l], row(mix_norm_pre[l]), win, row(q_latent_norm[l]), wq, conv_w[l], row(conv_b[l]), gb,
            cos, sin, tm)
        att = _dsa(qt, qit, k, vt, kidx, wt, attn_out_norm[l])
        hm = _mlstm(qm, kmt, vm, om, misc, gt, row(mlstm_out_norm[l]))
        x = _out_ffn(att, hm, x, mod[l], w_out[l].astype(bf16), row(mix_norm_post[l]), row(ffn_norm_pre[l]),
                     w_gate_up[l].astype(bf16), w_down[l].astype(bf16), row(ffn_norm_post[l]), tm)
    return x
```

```python
import functools

import numpy as np
import jax
import jax.numpy as jnp
from jax import lax
from jax.experimental import pallas as pl
from jax.experimental.pallas import tpu as pltpu

EPS = 1e-6
ROPE_THETA = 10000.0
ATT_HEADS = 8
HEAD_DIM = 64
ATT_WIDTH = ATT_HEADS * HEAD_DIM
Q_RANK = 256
IDX_HEADS = 8
ML_HEADS = 4
ML_DIM = 128
ML_WIDTH = ML_HEADS * ML_DIM
CONV_W = 4
TOPK = 256
QB = 128
LANE = 128
ML_CHUNK = 256
IN_SPLITS = (Q_RANK, ATT_WIDTH, ATT_WIDTH, HEAD_DIM, IDX_HEADS,
             ML_WIDTH, ML_WIDTH, ML_WIDTH, ML_WIDTH, ML_HEADS, ML_HEADS)

OFF_CQ = 0
OFF_K = OFF_CQ + Q_RANK
OFF_V = OFF_K + ATT_WIDTH
OFF_MISC = OFF_V + ATT_WIDTH
OFF_QKM = OFF_MISC + LANE
OFF_VM = OFF_QKM + 2 * ML_WIDTH
OFF_OM = OFF_VM + ML_WIDTH
IN_COLS_PADDED = OFF_OM + ML_WIDTH
MISC_W = HEAD_DIM
MISC_I = MISC_W + IDX_HEADS
MISC_F = MISC_I + ML_HEADS
MISC_END = MISC_F + ML_HEADS

VMEM_LIMIT = 56 * 1024 * 1024
NEG_BIG = -1e30
LOG2E = float(np.log2(np.e))
KEY_POS_INF = 0x7F800000
KEY_NEG_INF = (0xFF800000 - (1 << 32)) ^ 0x7FFFFFFF

bf16 = jnp.bfloat16
f32 = jnp.float32


def _rms(x, w):
    return x * lax.rsqrt(jnp.mean(x * x, axis=-1, keepdims=True) + EPS) * w


def _dot(a, b, precision=None):
    return jnp.dot(a, b, preferred_element_type=f32, precision=precision)


def _rope(x, cos, sin_signed):
    n = x.shape[1]
    lane = lax.broadcasted_iota(jnp.int32, x.shape, 1)
    low = (lane & (HEAD_DIM // 2)) == 0
    rot = jnp.where(low, pltpu.roll(x, n - HEAD_DIM // 2, 1), pltpu.roll(x, HEAD_DIM // 2, 1))
    return x * cos + rot * sin_signed


def _mod_kernel(c_ref, w_ref, b_ref, o_ref):
    c = c_ref[...]
    c_act = c * jax.nn.sigmoid(c)
    o_ref[0] = _dot(c_act, w_ref[0], precision=lax.Precision.HIGHEST) + b_ref[0]


def _modulation(c, w_mod, b_mod):
    depth, d, d6 = w_mod.shape
    b = c.shape[0]
    nj = d6 // d
    return pl.pallas_call(
        _mod_kernel,
        out_shape=jax.ShapeDtypeStruct((depth, b, d6), f32),
        grid=(depth, nj),
        in_specs=[pl.BlockSpec((b, d), lambda l, j: (0, 0)),
                  pl.BlockSpec((1, d, d), lambda l, j: (l, 0, j)),
                  pl.BlockSpec((1, 1, d), lambda l, j: (l, 0, j))],
        out_specs=pl.BlockSpec((1, b, d), lambda l, j: (l, 0, j)),
        name="modulation",
    )(c, w_mod, b_mod.reshape(depth, 1, d6))


def _rope_table_kernel(pos_ref, invf_ref, sign_ref, cos_ref, sin_ref):
    ang = pos_ref[0].astype(f32) * invf_ref[...]
    cos_ref[0] = jnp.cos(ang)
    sin_ref[0] = jnp.sin(ang) * sign_ref[...]


def _rope_tables(positions):
    b, s = positions.shape
    ts = min(s, 512)
    inv_freq = ROPE_THETA ** (-jnp.arange(0, HEAD_DIM, 2, dtype=f32) / HEAD_DIM)
    invf = jnp.tile(inv_freq, LANE // (HEAD_DIM // 2)).reshape(1, LANE)
    sign = np.where((np.arange(LANE) % HEAD_DIM) < HEAD_DIM // 2, -1.0, 1.0).astype(np.float32).reshape(1, LANE)
    tab = jax.ShapeDtypeStruct((b, s, LANE), f32)
    return pl.pallas_call(
        _rope_table_kernel,
        out_shape=(tab, tab),
        grid=(b, s // ts),
        in_specs=[pl.BlockSpec((1, ts, 1), lambda i, j: (i, j, 0)),
                  pl.BlockSpec((1, LANE), lambda i, j: (0, 0)),
                  pl.BlockSpec((1, LANE), lambda i, j: (0, 0))],
        out_specs=(pl.BlockSpec((1, ts, LANE), lambda i, j: (i, j, 0)),
                   pl.BlockSpec((1, ts, LANE), lambda i, j: (i, j, 0))),
        name="rope_tables",
    )(positions.reshape(b, s, 1), invf, jnp.asarray(sign))


def _in_proj_kernel(x_ref, mod_ref, nw_ref, win_ref, qln_ref, wq_ref, cw_ref, cb_ref, gb_ref, cos_ref, sin_ref,
                    qt_ref, qit_ref, k_ref, vt_ref, kidx_ref, wt_ref, gt_ref, qm_ref, km_ref, vm_ref, om_ref,
                    buf_ref):
    j = pl.program_id(1)
    tm = x_ref.shape[1]
    x = x_ref[0]
    sh = mod_ref[0, 0:1, :]
    sc = mod_ref[0, 1:2, :]
    hb = (_rms(x, nw_ref[...]) * (1.0 + sc) + sh).astype(bf16)

    cos = cos_ref[0]
    sin = sin_ref[0]
    reps = ATT_WIDTH // LANE
    cos_w = jnp.concatenate([cos] * reps, axis=1)
    sin_w = jnp.concatenate([sin] * reps, axis=1)

    cq = _dot(hb, win_ref[:, OFF_CQ:OFF_CQ + Q_RANK])
    cqn = _rms(cq, qln_ref[...]).astype(bf16)
    qq = _dot(cqn, wq_ref[...])
    q_t = (_rope(qq[:, :ATT_WIDTH], cos_w, sin_w) * (HEAD_DIM ** -0.5 * LOG2E)).T.astype(bf16)
    qi_t = _rope(qq[:, ATT_WIDTH:], cos_w, sin_w).T.astype(bf16)
    qt_ref[...] = jnp.zeros(qt_ref.shape, bf16)
    for rb in range(tm // QB):
        rows = slice(rb * QB, (rb + 1) * QB)
        for h in range(ATT_HEADS):
            d0 = (h % 2) * HEAD_DIM
            qt_ref[0, rb, h // 2, d0:d0 + HEAD_DIM, (h % 2) * QB:(h % 2 + 1) * QB] = (
                q_t[h * HEAD_DIM:(h + 1) * HEAD_DIM, rows])
        for h in range(IDX_HEADS):
            qit_ref[0, rb, :, h * QB:(h + 1) * QB] = qi_t[h * HEAD_DIM:(h + 1) * HEAD_DIM, rows]
    k_ref[0] = _rope(_dot(hb, win_ref[:, OFF_K:OFF_K + ATT_WIDTH]), cos_w, sin_w).astype(bf16)
    vt_ref[0] = _dot(hb, win_ref[:, OFF_V:OFF_V + ATT_WIDTH]).T.astype(bf16)

    m = _dot(hb, win_ref[:, OFF_MISC:OFF_MISC + LANE])
    lane = lax.broadcasted_iota(jnp.int32, m.shape, 1)
    roped = _rope(m, cos, sin)
    biased = m + gb_ref[...]
    logsig = jnp.minimum(biased, 0.0) - jnp.log1p(jnp.exp(-jnp.abs(biased)))
    w_scale = (IDX_HEADS * HEAD_DIM) ** -0.5
    misc = jnp.where(lane < MISC_W, roped,
                     jnp.where(lane < MISC_I, m * w_scale,
                               jnp.where(lane < MISC_F, biased,
                                         jnp.where(lane < MISC_END, logsig, 0.0))))
    misc_t = misc.T
    kidx_ref[0] = misc[:, 0:HEAD_DIM].astype(bf16)
    wt_ref[0] = misc_t[MISC_W:MISC_I, :]
    gt_ref[0] = misc_t[MISC_I:MISC_END, :]

    pre = _dot(hb, win_ref[:, OFF_QKM:OFF_QKM + 2 * ML_WIDTH])

    @pl.when(j == 0)
    def _():
        buf_ref[0:8, :] = jnp.zeros((8, 2 * ML_WIDTH), f32)

    buf_ref[8:8 + tm, :] = pre
    acc = jnp.broadcast_to(cb_ref[...], pre.shape)
    for t in range(CONV_W):
        acc = acc + cw_ref[t:t + 1, :] * buf_ref[8 - (CONV_W - 1) + t:8 - (CONV_W - 1) + t + tm, :]
    buf_ref[0:8, :] = buf_ref[tm:tm + 8, :]
    qk = acc * jax.nn.sigmoid(acc)
    qm_ref[0] = (qk[:, :ML_WIDTH] * (ML_DIM ** -0.5)).astype(bf16)
    km_ref[0] = qk[:, ML_WIDTH:].astype(bf16)
    vm_ref[0] = _dot(hb, win_ref[:, OFF_VM:OFF_VM + ML_WIDTH]).astype(bf16)
    om_ref[0] = _dot(hb, win_ref[:, OFF_OM:OFF_OM + ML_WIDTH])


def _in_proj(x, mod_l, nw, win, qln, wq, cw, cb, gb, cos, sin, tm):
    b, s, d = x.shape
    nt = s // tm

    def tok(w):
        return pl.BlockSpec((1, tm, w), lambda i, j: (i, j, 0))

    def tok_t(r):
        return pl.BlockSpec((1, r, tm), lambda i, j: (i, 0, j))

    def full(a):
        return pl.BlockSpec(a.shape, lambda i, j: (0,) * a.ndim)

    sd = jax.ShapeDtypeStruct
    n_rb, rb_t = s // QB, tm // QB
    qt_shape = (ATT_HEADS // 2, 2 * HEAD_DIM, 2 * QB)
    qit_shape = (HEAD_DIM, IDX_HEADS * QB)
    out_shape = (sd((b, n_rb) + qt_shape, bf16), sd((b, n_rb) + qit_shape, bf16), sd((b, s, ATT_WIDTH), bf16),
                 sd((b, ATT_WIDTH, s), bf16), sd((b, s, HEAD_DIM), bf16), sd((b, IDX_HEADS, s), f32),
                 sd((b, 2 * ML_HEADS, s), f32), sd((b, s, ML_WIDTH), bf16),
                 sd((b, s, ML_WIDTH), bf16), sd((b, s, ML_WIDTH), bf16), sd((b, s, ML_WIDTH), f32))
    out_specs = (pl.BlockSpec((1, rb_t) + qt_shape, lambda i, j: (i, j, 0, 0, 0)),
                 pl.BlockSpec((1, rb_t) + qit_shape, lambda i, j: (i, j, 0, 0)),
                 tok(ATT_WIDTH), tok_t(ATT_WIDTH), tok(HEAD_DIM), tok_t(IDX_HEADS),
                 tok_t(2 * ML_HEADS), tok(ML_WIDTH), tok(ML_WIDTH), tok(ML_WIDTH), tok(ML_WIDTH))
    return pl.pallas_call(
        _in_proj_kernel,
        out_shape=out_shape,
        grid=(b, nt),
        in_specs=[tok(d), pl.BlockSpec((1, 6, d), lambda i, j: (i, 0, 0)), full(nw), full(win), full(qln), full(wq),
                  full(cw), full(cb), full(gb), tok(LANE), tok(LANE)],
        out_specs=out_specs,
        scratch_shapes=[pltpu.VMEM((tm + 8, 2 * ML_WIDTH), f32)],
        compiler_params=pltpu.CompilerParams(dimension_semantics=("arbitrary", "arbitrary"),
                                             vmem_limit_bytes=VMEM_LIMIT),
        name="in_proj",
    )(x, mod_l, nw, win, qln, wq, cw, cb, gb, cos, sin)


def _key_to_f32(key):
    bits = jnp.where(key >= 0, key, key ^ jnp.int32(0x7FFFFFFF))
    return lax.bitcast_convert_type(bits, f32)


def _fold8(x, op):
    parts = [x[i * 8:(i + 1) * 8] for i in range(x.shape[0] // 8)]
    while len(parts) > 1:
        parts = [op(parts[i], parts[i + 1]) for i in range(0, len(parts), 2)]
    return parts[0]


def _dsa_kernel(qit_ref, kidx_ref, wt_ref, qt_ref, k_ref, vt_ref, nw_ref, o_ref, sc_ref, lo_ref, hi_ref, *, n_sel):
    s_len = k_ref.shape[1]
    n_rb = s_len // QB
    sup = 4 * QB
    kf = float(n_sel)
    key_i = lax.broadcasted_iota(jnp.int32, (QB, QB), 0)
    row_i = lax.broadcasted_iota(jnp.int32, (QB, QB), 1)

    def col_sum(x):
        return jnp.sum(x, axis=0, keepdims=True)

    def score_rows(r, carry):
        roff = pl.multiple_of(r * QB, QB)
        qit = qit_ref[0, r]
        w = wt_ref[0, :, pl.ds(roff, QB)]

        def score_keys(c, carry2):
            for p in range(sup // QB):
                off = pl.multiple_of(c * sup + p * QB, QB)
                lg = _dot(kidx_ref[0, pl.ds(off, QB), :], qit)
                acc = jnp.zeros((QB, QB), f32)
                for h in range(IDX_HEADS):
                    acc = acc + w[h:h + 1, :] * jnp.maximum(lg[:, h * QB:(h + 1) * QB], 0.0)
                sc_ref[pl.ds(off, QB), pl.ds(roff, QB)] = jnp.where(off + key_i <= roff + row_i, acc, -jnp.inf)
            return carry2

        lax.fori_loop(0, r // 4 + 1, score_keys, 0)
        return carry

    lax.fori_loop(0, n_rb, score_rows, 0)

    lo_ref[...] = jnp.full(lo_ref.shape, KEY_NEG_INF, jnp.int32)
    hi_ref[...] = jnp.full(hi_ref.shape, KEY_POS_INF, jnp.int32)
    first = n_sel // QB

    def bisect(_, carry):
        for r in range(first, n_rb):
            cols = slice(r * QB, (r + 1) * QB)
            lo = lo_ref[r, 0:1, :]
            hi = hi_ref[r, 0:1, :]
            mid = (lo & hi) + ((lo ^ hi) >> 1)
            thr = _key_to_f32(mid)
            part = jnp.zeros((QB, QB), f32)
            for c in range(r + 1):
                part = part + jnp.where(sc_ref[c * QB:(c + 1) * QB, cols] >= thr, 1.0, 0.0)
            ge = col_sum(part) >= kf
            lo_ref[r, 0:1, :] = jnp.where(ge, mid, lo)
            hi_ref[r, 0:1, :] = jnp.where(ge, hi, mid)
        return carry

    lax.fori_loop(0, 32, bisect, 0)

    def finalize(r, carry):
        roff = pl.multiple_of(r * QB, QB)
        thr = _key_to_f32(lo_ref[r, 0:1, :])

        def count(pred):
            def body(c, part):
                off = pl.multiple_of(c * QB, QB)
                blk = sc_ref[pl.ds(off, QB), pl.ds(roff, QB)]
                return part + jnp.where(pred(blk, off + key_i), 1.0, 0.0)
            return col_sum(lax.fori_loop(0, r + 1, body, jnp.zeros((QB, QB), f32)))

        finite = thr > -jnp.inf
        tied = jnp.logical_and(count(lambda blk, pos: blk >= thr) != kf, finite)
        cut0 = jnp.where(finite, s_len, -1)

        def break_ties(_):
            need = kf - count(lambda blk, pos: blk > thr)

            def bisect_idx(_, lh):
                lo_j, hi_j = lh
                mid = (lo_j + hi_j) >> 1
                ok = count(lambda blk, pos: jnp.logical_and(blk == thr, pos <= mid)) >= need
                return jnp.where(ok, lo_j, mid), jnp.where(ok, mid, hi_j)

            n_it = int(np.ceil(np.log2(s_len + 1)))
            _, hi_j = lax.fori_loop(0, n_it, bisect_idx, (jnp.full((1, QB), -1, jnp.int32),
                                                          jnp.full((1, QB), s_len - 1, jnp.int32)))
            return jnp.where(tied, hi_j, cut0)

        cut = lax.cond(jnp.max(tied.astype(jnp.int32)) > 0, break_ties, lambda _: cut0, 0)

        def write_mask(c, carry2):
            off = pl.multiple_of(c * QB, QB)
            blk = sc_ref[pl.ds(off, QB), pl.ds(roff, QB)]
            sel = jnp.logical_or(blk > thr, jnp.logical_and(blk == thr, off + key_i <= cut))
            sc_ref[pl.ds(off, QB), pl.ds(roff, QB)] = jnp.where(sel, 0.0, NEG_BIG)
            return carry2

        lax.fori_loop(0, 4 * (r // 4 + 1), write_mask, 0)
        return carry

    lax.fori_loop(0, n_rb, finalize, 0)

    def attend(r, carry):
        roff = pl.multiple_of(r * QB, QB)
        n_sup = r // 4 + 1

        def logits(c):
            off = pl.multiple_of(c * sup, sup)
            bias = sc_ref[pl.ds(off, sup), pl.ds(roff, QB)]
            out = []
            for hp in range(ATT_HEADS // 2):
                k_pair = k_ref[0, pl.ds(off, sup), hp * 2 * HEAD_DIM:(hp + 1) * 2 * HEAD_DIM]
                s2 = _dot(k_pair, qt_ref[0, r, hp])
                out += [s2[:, :QB] + bias, s2[:, QB:] + bias]
            return out, off

        def key_step(c, carry2):
            m_run, acc = carry2
            s_all, off = logits(c)
            m_new, acc_new = [], []
            for h in range(ATT_HEADS):
                m_h = jnp.maximum(m_run[h], jnp.max(_fold8(s_all[h], jnp.maximum), axis=0, keepdims=True))
                alpha = jnp.exp2(m_run[h] - m_h)
                p = jnp.exp2(s_all[h] - m_h).astype(bf16)
                vt_h = jnp.concatenate([vt_ref[0, h * HEAD_DIM:(h + 1) * HEAD_DIM, pl.ds(off, sup)], ones_rows], axis=0)
                acc_new.append(alpha * acc[h] + _dot(vt_h, p))
                m_new.append(m_h)
            return tuple(m_new), tuple(acc_new)

        ones_rows = jnp.ones((16, sup), bf16)
        _, acc = lax.fori_loop(0, n_sup, key_step,
                               (tuple(jnp.full((1, QB), NEG_BIG, f32) for _ in range(ATT_HEADS)),
                                tuple(jnp.zeros((HEAD_DIM + 16, QB), f32) for _ in range(ATT_HEADS))))
        outs = []
        for h in range(ATT_HEADS):
            o = acc[h][:HEAD_DIM] / acc[h][HEAD_DIM:HEAD_DIM + 1]
            inv = lax.rsqrt(jnp.mean(o * o, axis=0, keepdims=True) + EPS)
            outs.append(o * inv * nw_ref[h * HEAD_DIM:(h + 1) * HEAD_DIM, :])
        o_ref[0, pl.ds(roff, QB), :] = jnp.concatenate(outs, axis=0).T.astype(bf16)
        return carry

    lax.fori_loop(0, n_rb, attend, 0)


def _dsa(qt, qit, k, vt, kidx, wt, nw):
    b, s, _ = k.shape
    n_rb = s // QB
    n_sel = min(TOPK, s // 4)
    s_pad = -(-s // (4 * QB)) * (4 * QB)
    kernel = functools.partial(_dsa_kernel, n_sel=n_sel)

    def per_batch(a):
        return pl.BlockSpec((1,) + a.shape[1:], lambda i: (i,) + (0,) * (a.ndim - 1))

    nw_col = nw.reshape(ATT_WIDTH, 1)
    return pl.pallas_call(
        kernel,
        out_shape=jax.ShapeDtypeStruct((b, s, ATT_WIDTH), bf16),
        grid=(b,),
        in_specs=[per_batch(qit), per_batch(kidx), per_batch(wt), per_batch(qt), per_batch(k), per_batch(vt),
                  pl.BlockSpec((ATT_WIDTH, 1), lambda i: (0, 0))],
        out_specs=pl.BlockSpec((1, s, ATT_WIDTH), lambda i: (i, 0, 0)),
        scratch_shapes=[pltpu.VMEM((s_pad, s), f32),
                        pltpu.VMEM((n_rb, 8, QB), jnp.int32),
                        pltpu.VMEM((n_rb, 8, QB), jnp.int32)],
        compiler_params=pltpu.CompilerParams(dimension_semantics=("arbitrary",),
                                             vmem_limit_bytes=VMEM_LIMIT),
        name="dsa_attention",
    )(qit, kidx, wt, qt, k, vt, nw_col)


ML_STATE_ROWS = ML_DIM + 16


def _mlstm_kernel(qm_ref, km_ref, vm_ref, om_ref, gt_ref, nw_ref, hm_ref, c_ref, *, chunk):
    s_len = qm_ref.shape[1]
    n_chunks = s_len // chunk
    s_i = lax.broadcasted_iota(jnp.int32, (chunk, chunk), 0)
    t_i = lax.broadcasted_iota(jnp.int32, (chunk, chunk), 1)
    causal = s_i <= t_i
    tri_t = jnp.where(causal, 1.0, 0.0).astype(f32)
    first_row = lax.broadcasted_iota(jnp.int32, (ML_STATE_ROWS - ML_DIM, chunk), 0) == 0
    c_ref[...] = jnp.zeros(c_ref.shape, f32)

    def body(c, m_state):
        off = pl.multiple_of(c * chunk, chunk)
        g_row = gt_ref[0, :, pl.ds(off, chunk)]
        cum_row = _dot(g_row, tri_t, precision=lax.Precision.HIGHEST)
        g_src = g_row[:ML_HEADS] - cum_row[ML_HEADS:]
        g_src_col = jnp.concatenate([g_src, jnp.zeros((LANE - ML_HEADS, chunk), f32)], axis=0).T
        m_next = []
        for h in range(ML_HEADS):
            ig_row = g_row[h:h + 1, :]
            b_row = cum_row[ML_HEADS + h:ML_HEADS + h + 1, :]
            m_s = m_state[h]
            total = b_row[:, chunk - 1:chunk]
            hs = slice(h * ML_DIM, (h + 1) * ML_DIM)
            q_t = qm_ref[0, pl.ds(off, chunk), hs].astype(f32).T.astype(bf16)
            k_h = km_ref[0, pl.ds(off, chunk), hs]
            v_tf = vm_ref[0, pl.ds(off, chunk), hs].astype(f32).T

            d_t = jnp.where(causal, g_src_col[:, h:h + 1] + b_row, -jnp.inf)
            inter = b_row + m_s
            m_row = jnp.maximum(inter, jnp.max(d_t, axis=0, keepdims=True))
            w_inter = jnp.exp(inter - m_row)
            s_t = _dot(k_h, q_t) * jnp.exp(d_t - m_row)
            c_ext = c_ref[h]
            qc = _dot(c_ext.astype(bf16), q_t)
            num = _dot(v_tf.astype(bf16), s_t.astype(bf16)) + w_inter * qc[:ML_DIM]
            den = jnp.sum(s_t, axis=0, keepdims=True) + w_inter * qc[ML_DIM:ML_DIM + 1]
            hh = num / jnp.maximum(jnp.abs(den), jnp.exp(-m_row))

            w_log = total - b_row + ig_row
            m_new = jnp.maximum(total + m_s, jnp.max(w_log, axis=1, keepdims=True))
            decay = jnp.exp(total + m_s - m_new)
            w_state = jnp.exp(w_log - m_new)
            lhs = jnp.concatenate([(v_tf * w_state).astype(bf16),
                                   jnp.where(first_row, w_state, 0.0).astype(bf16)], axis=0)
            c_ref[h] = decay * c_ext + _dot(lhs, k_h)
            m_next.append(m_new)

            inv = lax.rsqrt(jnp.mean(hh * hh, axis=0, keepdims=True) + EPS)
            gate = jax.nn.sigmoid(om_ref[0, pl.ds(off, chunk), hs].T)
            out_t = gate * hh * inv * nw_ref[hs, :]
            hm_ref[0, pl.ds(off, chunk), hs] = out_t.T.astype(bf16)
        return tuple(m_next)

    lax.fori_loop(0, n_chunks, body, tuple(jnp.zeros((1, 1), f32) for _ in range(ML_HEADS)))


def _mlstm(qm, km, vm, om, gt, nw):
    b, s, _ = qm.shape
    chunk = min(ML_CHUNK, s)
    kernel = functools.partial(_mlstm_kernel, chunk=chunk)
    tok = pl.BlockSpec((1, s, ML_WIDTH), lambda i: (i, 0, 0))
    return pl.pallas_call(
        kernel,
        out_shape=jax.ShapeDtypeStruct((b, s, ML_WIDTH), bf16),
        grid=(b,),
        in_specs=[tok, tok, tok, tok,
                  pl.BlockSpec((1, 2 * ML_HEADS, s), lambda i: (i, 0, 0)),
                  pl.BlockSpec((ML_WIDTH, 1), lambda i: (0, 0))],
        out_specs=tok,
        scratch_shapes=[pltpu.VMEM((ML_HEADS, ML_STATE_ROWS, ML_DIM), f32)],
        compiler_params=pltpu.CompilerParams(dimension_semantics=("arbitrary",),
                                             vmem_limit_bytes=VMEM_LIMIT),
        name="mlstm",
    )(qm, km, vm, om, gt, nw.reshape(ML_WIDTH, 1))


def _out_ffn_kernel(att_ref, hm_ref, x_ref, mod_ref, wout_ref, npost_ref, fpre_ref, wgu_ref, wdn_ref, fpost_ref,
                    o_ref, *, bounds):
    hid = wdn_ref.shape[0]
    g_m = mod_ref[0, 2:3, :]
    sh_f = mod_ref[0, 3:4, :]
    sc_f = mod_ref[0, 4:5, :]
    g_f = mod_ref[0, 5:6, :]
    y = _dot(att_ref[0], wout_ref[0:ATT_WIDTH, :]) + _dot(hm_ref[0], wout_ref[ATT_WIDTH:, :])
    x1 = x_ref[0] + g_m * _rms(y, npost_ref[...])
    h2 = (_rms(x1, fpre_ref[...]) * (1.0 + sc_f) + sh_f).astype(bf16)
    y2 = jnp.zeros(x1.shape, f32)
    for c0, c1 in zip(bounds[:-1], bounds[1:]):
        g = _dot(h2, wgu_ref[:, c0:c1])
        u = _dot(h2, wgu_ref[:, hid + c0:hid + c1])
        a = (g * jax.nn.sigmoid(g) * u).astype(bf16)
        y2 = y2 + _dot(a, wdn_ref[c0:c1, :])
    o_ref[0] = x1 + g_f * _rms(y2, fpost_ref[...])


def _out_ffn(att, hm, x, mod_l, wout, npost, fpre, wgu, wdn, fpost, tm):
    b, s, d = x.shape
    hid = wdn.shape[0]
    mxu_tile = 256
    cut = -(-(hid // 2) // mxu_tile) * mxu_tile
    bounds = (0, cut, hid) if 0 < cut < hid else (0, hid)

    def tok(w):
        return pl.BlockSpec((1, tm, w), lambda i, j: (i, j, 0))

    def const(a):
        return pl.BlockSpec(a.shape, lambda i, j: (0,) * a.ndim, pipeline_mode=pl.Buffered(1))

    kernel = functools.partial(_out_ffn_kernel, bounds=bounds)
    return pl.pallas_call(
        kernel,
        out_shape=jax.ShapeDtypeStruct((b, s, d), f32),
        grid=(b, s // tm),
        in_specs=[tok(ATT_WIDTH), tok(ML_WIDTH), tok(d), pl.BlockSpec((1, 6, d), lambda i, j: (i, 0, 0)),
                  const(wout), const(npost), const(fpre), const(wgu), const(wdn), const(fpost)],
        out_specs=tok(d),
        compiler_params=pltpu.CompilerParams(dimension_semantics=("arbitrary", "arbitrary"),
                                             vmem_limit_bytes=VMEM_LIMIT),
        name="out_ffn",
    )(att, hm, x, mod_l, wout, npost, fpre, wgu, wdn, fpost)


def _arrange_w_in(w):
    d = w.shape[0]
    pts = [int(p) for p in np.cumsum(IN_SPLITS)[:-1]]
    cq, k_att, v_att, k_idx, w_idx, q_m, k_m, v_m, o_m, i_m, f_m = jnp.split(w, pts, axis=1)
    misc = jnp.concatenate([k_idx, w_idx, i_m, f_m, jnp.zeros((d, LANE - MISC_END), w.dtype)], axis=1)
    return jnp.concatenate([cq, k_att, v_att, misc, q_m, k_m, v_m, o_m], axis=1).astype(bf16)


def kernel(x, c, positions, w_mod, b_mod, mix_norm_pre, mix_norm_post, w_in, q_latent_norm, w_q_up, w_qidx_up,
           conv_w, conv_b, b_igate, b_fgate, attn_out_norm, mlstm_out_norm, w_out, ffn_norm_pre, ffn_norm_post,
           w_gate_up, w_down):
    depth = w_mod.shape[0]
    b, s, d = x.shape
    tm = min(512, s)
    mod = _modulation(c, w_mod, b_mod).reshape(depth, b, 6, d)
    cos, sin = _rope_tables(positions)

    def row(a):
        return a.reshape(1, -1)

    for l in range(depth):
        win = _arrange_w_in(w_in[l])
        wq = jnp.concatenate([w_q_up[l], w_qidx_up[l]], axis=1).astype(bf16)
        gb = jnp.zeros((1, LANE), f32).at[0, MISC_I:MISC_F].set(b_igate[l]).at[0, MISC_F:MISC_END].set(b_fgate[l])
        qt, qit, k, vt, kidx, wt, gt, qm, km, vm, om = _in_proj(
            x, mod[l], row(mix_norm_pre[l]), win, row(q_latent_norm[l]), wq, conv_w[l], row(conv_b[l]), gb,
            cos, sin, tm)
        att = _dsa(qt, qit, k, vt, kidx, wt, attn_out_norm[l])
        hm = _mlstm(qm, km, vm, om, gt, mlstm_out_norm[l])
        x = _out_ffn(att, hm, x, mod[l], w_out[l].astype(bf16), row(mix_norm_post[l]), row(ffn_norm_pre[l]),
                     w_gate_up[l].astype(bf16), w_down[l].astype(bf16), row(ffn_norm_post[l]), tm)
    return x
```

```python
import functools

import numpy as np
import jax
import jax.numpy as jnp
from jax import lax
from jax.experimental import pallas as pl
from jax.experimental.pallas import tpu as pltpu

EPS = 1e-6
ROPE_THETA = 10000.0
ATT_HEADS = 8
HEAD_DIM = 64
ATT_WIDTH = ATT_HEADS * HEAD_DIM
Q_RANK = 256
IDX_HEADS = 8
ML_HEADS = 4
ML_DIM = 128
ML_WIDTH = ML_HEADS * ML_DIM
CONV_W = 4
TOPK = 256
QB = 128
LANE = 128
ML_CHUNK = 256
IN_SPLITS = (Q_RANK, ATT_WIDTH, ATT_WIDTH, HEAD_DIM, IDX_HEADS,
             ML_WIDTH, ML_WIDTH, ML_WIDTH, ML_WIDTH, ML_HEADS, ML_HEADS)

OFF_CQ = 0
OFF_K = OFF_CQ + Q_RANK
OFF_V = OFF_K + ATT_WIDTH
OFF_MISC = OFF_V + ATT_WIDTH
OFF_QKM = OFF_MISC + LANE
OFF_VM = OFF_QKM + 2 * ML_WIDTH
OFF_OM = OFF_VM + ML_WIDTH
IN_COLS_PADDED = OFF_OM + ML_WIDTH
MISC_W = HEAD_DIM
MISC_I = MISC_W + IDX_HEADS
MISC_F = MISC_I + ML_HEADS
MISC_END = MISC_F + ML_HEADS

VMEM_LIMIT = 56 * 1024 * 1024
NEG_BIG = -1e30
LOG2E = float(np.log2(np.e))
KEY_POS_INF = 0x7F800000
KEY_NEG_INF = (0xFF800000 - (1 << 32)) ^ 0x7FFFFFFF

bf16 = jnp.bfloat16
f32 = jnp.float32


def _rms(x, w):
    return x * lax.rsqrt(jnp.mean(x * x, axis=-1, keepdims=True) + EPS) * w


def _dot(a, b, precision=None):
    return jnp.dot(a, b, preferred_element_type=f32, precision=precision)


def _rope(x, cos, sin_signed):
    n = x.shape[1]
    lane = lax.broadcasted_iota(jnp.int32, x.shape, 1)
    low = (lane & (HEAD_DIM // 2)) == 0
    rot = jnp.where(low, pltpu.roll(x, n - HEAD_DIM // 2, 1), pltpu.roll(x, HEAD_DIM // 2, 1))
    return x * cos + rot * sin_signed


def _mod_kernel(c_ref, w_ref, b_ref, o_ref):
    c = c_ref[...]
    c_act = c * jax.nn.sigmoid(c)
    o_ref[0] = _dot(c_act, w_ref[0], precision=lax.Precision.HIGHEST) + b_ref[0]


def _modulation(c, w_mod, b_mod):
    depth, d, d6 = w_mod.shape
    b = c.shape[0]
    nj = d6 // d
    return pl.pallas_call(
        _mod_kernel,
        out_shape=jax.ShapeDtypeStruct((depth, b, d6), f32),
        grid=(depth, nj),
        in_specs=[pl.BlockSpec((b, d), lambda l, j: (0, 0)),
                  pl.BlockSpec((1, d, d), lambda l, j: (l, 0, j)),
                  pl.BlockSpec((1, 1, d), lambda l, j: (l, 0, j))],
        out_specs=pl.BlockSpec((1, b, d), lambda l, j: (l, 0, j)),
        name="modulation",
    )(c, w_mod, b_mod.reshape(depth, 1, d6))


def _rope_table_kernel(pos_ref, invf_ref, sign_ref, cos_ref, sin_ref):
    ang = pos_ref[0].astype(f32) * invf_ref[...]
    cos_ref[0] = jnp.cos(ang)
    sin_ref[0] = jnp.sin(ang) * sign_ref[...]


def _rope_tables(positions):
    b, s = positions.shape
    ts = min(s, 512)
    inv_freq = ROPE_THETA ** (-jnp.arange(0, HEAD_DIM, 2, dtype=f32) / HEAD_DIM)
    invf = jnp.tile(inv_freq, LANE // (HEAD_DIM // 2)).reshape(1, LANE)
    sign = np.where((np.arange(LANE) % HEAD_DIM) < HEAD_DIM // 2, -1.0, 1.0).astype(np.float32).reshape(1, LANE)
    tab = jax.ShapeDtypeStruct((b, s, LANE), f32)
    return pl.pallas_call(
        _rope_table_kernel,
        out_shape=(tab, tab),
        grid=(b, s // ts),
        in_specs=[pl.BlockSpec((1, ts, 1), lambda i, j: (i, j, 0)),
                  pl.BlockSpec((1, LANE), lambda i, j: (0, 0)),
                  pl.BlockSpec((1, LANE), lambda i, j: (0, 0))],
        out_specs=(pl.BlockSpec((1, ts, LANE), lambda i, j: (i, j, 0)),
                   pl.BlockSpec((1, ts, LANE), lambda i, j: (i, j, 0))),
        name="rope_tables",
    )(positions.reshape(b, s, 1), invf, jnp.asarray(sign))


def _in_proj_kernel(x_ref, mod_ref, nw_ref, win_ref, qln_ref, wq_ref, cw_ref, cb_ref, gb_ref, cos_ref, sin_ref,
                    qt_ref, qit_ref, k_ref, vt_ref, kidx_ref, wt_ref, gt_ref, qm_ref, km_ref, vm_ref, om_ref,
                    buf_ref):
    tm = x_ref.shape[1]

    @pl.when(pl.program_id(1) == 0)
    def _():
        buf_ref[0:8, :] = jnp.zeros((8, 2 * ML_WIDTH), f32)

    x = x_ref[0]
    sh = mod_ref[0, 0:1, :]
    sc = mod_ref[0, 1:2, :]
    hb = (_rms(x, nw_ref[...]) * (1.0 + sc) + sh).astype(bf16)

    cos = cos_ref[0]
    sin = sin_ref[0]
    reps = ATT_WIDTH // LANE
    cos_w = jnp.concatenate([cos] * reps, axis=1)
    sin_w = jnp.concatenate([sin] * reps, axis=1)

    cq = _dot(hb, win_ref[:, OFF_CQ:OFF_CQ + Q_RANK])
    cqn = _rms(cq, qln_ref[...]).astype(bf16)
    qq = _dot(cqn, wq_ref[...])
    q_t = (_rope(qq[:, :ATT_WIDTH], cos_w, sin_w) * (HEAD_DIM ** -0.5 * LOG2E)).T.astype(bf16)
    qi_t = _rope(qq[:, ATT_WIDTH:], cos_w, sin_w).T.astype(bf16)
    qt_ref[...] = jnp.zeros(qt_ref.shape, bf16)
    for h in range(ATT_HEADS):
        d0 = (h % 2) * HEAD_DIM
        qt_ref[0, 0, h // 2, d0:d0 + HEAD_DIM, (h % 2) * tm:(h % 2 + 1) * tm] = q_t[h * HEAD_DIM:(h + 1) * HEAD_DIM, :]
    for h in range(IDX_HEADS):
        qit_ref[0, 0, :, h * tm:(h + 1) * tm] = qi_t[h * HEAD_DIM:(h + 1) * HEAD_DIM, :]
    k_ref[0] = _rope(_dot(hb, win_ref[:, OFF_K:OFF_K + ATT_WIDTH]), cos_w, sin_w).astype(bf16)
    vt_ref[0] = _dot(hb, win_ref[:, OFF_V:OFF_V + ATT_WIDTH]).T.astype(bf16)

    m = _dot(hb, win_ref[:, OFF_MISC:OFF_MISC + LANE])
    lane = lax.broadcasted_iota(jnp.int32, m.shape, 1)
    roped = _rope(m, cos, sin)
    biased = m + gb_ref[...]
    logsig = jnp.minimum(biased, 0.0) - jnp.log1p(jnp.exp(-jnp.abs(biased)))
    w_scale = (IDX_HEADS * HEAD_DIM) ** -0.5
    misc = jnp.where(lane < MISC_W, roped,
                     jnp.where(lane < MISC_I, m * w_scale,
                               jnp.where(lane < MISC_F, biased,
                                         jnp.where(lane < MISC_END, logsig, 0.0))))
    misc_t = misc.T
    kidx_ref[0] = misc[:, 0:HEAD_DIM].astype(bf16)
    wt_ref[0] = misc_t[MISC_W:MISC_I, :]
    gt_ref[0] = misc_t[MISC_I:MISC_END, :]

    pre = _dot(hb, win_ref[:, OFF_QKM:OFF_QKM + 2 * ML_WIDTH])
    buf_ref[8:8 + tm, :] = pre
    acc = jnp.broadcast_to(cb_ref[...], pre.shape)
    for t in range(CONV_W):
        acc = acc + cw_ref[t:t + 1, :] * buf_ref[8 - (CONV_W - 1) + t:8 - (CONV_W - 1) + t + tm, :]
    buf_ref[0:8, :] = buf_ref[tm:tm + 8, :]
    qk = acc * jax.nn.sigmoid(acc)
    qm_ref[0] = (qk[:, :ML_WIDTH] * (ML_DIM ** -0.5)).astype(bf16)
    km_ref[0] = qk[:, ML_WIDTH:].astype(bf16)
    vm_ref[0] = _dot(hb, win_ref[:, OFF_VM:OFF_VM + ML_WIDTH]).astype(bf16)
    om_ref[0] = _dot(hb, win_ref[:, OFF_OM:OFF_OM + ML_WIDTH])


def _in_proj(x, mod_l, nw, win, qln, wq, cw, cb, gb, cos, sin, tm):
    b, s, d = x.shape
    nt = s // tm

    def tok(w):
        return pl.BlockSpec((1, tm, w), lambda i, j: (i, j, 0))

    def tok_t(r):
        return pl.BlockSpec((1, r, tm), lambda i, j: (i, 0, j))

    def full(a):
        return pl.BlockSpec(a.shape, lambda i, j: (0,) * a.ndim)

    sd = jax.ShapeDtypeStruct
    qt_shape = (ATT_HEADS // 2, 2 * HEAD_DIM, 2 * tm)
    qit_shape = (HEAD_DIM, IDX_HEADS * tm)
    out_shape = (sd((b, nt) + qt_shape, bf16), sd((b, nt) + qit_shape, bf16), sd((b, s, ATT_WIDTH), bf16),
                 sd((b, ATT_WIDTH, s), bf16), sd((b, s, HEAD_DIM), bf16), sd((b, IDX_HEADS, s), f32),
                 sd((b, 2 * ML_HEADS, s), f32), sd((b, s, ML_WIDTH), bf16),
                 sd((b, s, ML_WIDTH), bf16), sd((b, s, ML_WIDTH), bf16), sd((b, s, ML_WIDTH), f32))
    out_specs = (pl.BlockSpec((1, 1) + qt_shape, lambda i, j: (i, j, 0, 0, 0)),
                 pl.BlockSpec((1, 1) + qit_shape, lambda i, j: (i, j, 0, 0)),
                 tok(ATT_WIDTH), tok_t(ATT_WIDTH), tok(HEAD_DIM), tok_t(IDX_HEADS),
                 tok_t(2 * ML_HEADS), tok(ML_WIDTH), tok(ML_WIDTH), tok(ML_WIDTH), tok(ML_WIDTH))
    return pl.pallas_call(
        _in_proj_kernel,
        out_shape=out_shape,
        grid=(b, nt),
        in_specs=[tok(d), pl.BlockSpec((1, 6, d), lambda i, j: (i, 0, 0)), full(nw), full(win), full(qln), full(wq),
                  full(cw), full(cb), full(gb), tok(LANE), tok(LANE)],
        out_specs=out_specs,
        scratch_shapes=[pltpu.VMEM((tm + 8, 2 * ML_WIDTH), f32)],
        compiler_params=pltpu.CompilerParams(dimension_semantics=("arbitrary", "arbitrary"),
                                             vmem_limit_bytes=VMEM_LIMIT),
        name="in_proj",
    )(x, mod_l, nw, win, qln, wq, cw, cb, gb, cos, sin)


def _key_to_f32(key):
    bits = jnp.where(key >= 0, key, key ^ jnp.int32(0x7FFFFFFF))
    return lax.bitcast_convert_type(bits, f32)


def _fold8(x, op):
    parts = [x[i * 8:(i + 1) * 8] for i in range(x.shape[0] // 8)]
    while len(parts) > 1:
        parts = [op(parts[i], parts[i + 1]) for i in range(0, len(parts), 2)]
    return parts[0]


def _dsa_kernel(qit_ref, kidx_ref, wt_ref, qt_ref, k_ref, vt_ref, nw_ref, o_ref, sc_ref, lo_ref, hi_ref, *, n_sel):
    s_len = k_ref.shape[1]
    n_rb = s_len // QB
    rg = qt_ref.shape[-1] // 2
    n_rg = s_len // rg
    sup = rg
    kf = float(n_sel)
    key_i = lax.broadcasted_iota(jnp.int32, (QB, QB), 0)
    row_i = lax.broadcasted_iota(jnp.int32, (QB, QB), 1)
    key_g = lax.broadcasted_iota(jnp.int32, (QB, rg), 0)
    row_g = lax.broadcasted_iota(jnp.int32, (QB, rg), 1)

    def col_sum(x):
        return jnp.sum(x, axis=0, keepdims=True)

    def score_rows(g, carry):
        goff = pl.multiple_of(g * rg, rg)
        w = wt_ref[0, :, pl.ds(goff, rg)]

        def score_keys(c, carry2):
            for p in range(sup // QB):
                off = pl.multiple_of(c * sup + p * QB, QB)
                lg = _dot(kidx_ref[0, pl.ds(off, QB), :], qit_ref[0, g])
                acc = jnp.zeros((QB, rg), f32)
                for h in range(IDX_HEADS):
                    acc = acc + w[h:h + 1, :] * jnp.maximum(lg[:, h * rg:(h + 1) * rg], 0.0)
                sc_ref[pl.ds(off, QB), pl.ds(goff, rg)] = jnp.where(off + key_g <= goff + row_g, acc, -jnp.inf)
            return carry2

        lax.fori_loop(0, g + 1, score_keys, 0)
        return carry

    lax.fori_loop(0, n_rg, score_rows, 0)

    lo_ref[...] = jnp.full(lo_ref.shape, KEY_NEG_INF, jnp.int32)
    hi_ref[...] = jnp.full(hi_ref.shape, KEY_POS_INF, jnp.int32)
    first = n_sel // QB
    ones_lhs = jnp.ones((16, s_len), bf16)

    def bisect(_, carry):
        for r in range(first, n_rb):
            lo = lo_ref[r, 0:1, :]
            hi = hi_ref[r, 0:1, :]
            mid = (lo & hi) + ((lo ^ hi) >> 1)
            thr = _key_to_f32(mid)
            hit = jnp.where(sc_ref[0:(r + 1) * QB, r * QB:(r + 1) * QB] >= thr, 1.0, 0.0).astype(bf16)
            ge = _dot(ones_lhs[:, :(r + 1) * QB], hit)[0:1] >= kf
            lo_ref[r, 0:1, :] = jnp.where(ge, mid, lo)
            hi_ref[r, 0:1, :] = jnp.where(ge, hi, mid)
        return carry

    lax.fori_loop(0, 32, bisect, 0)

    def finalize(r, carry):
        roff = pl.multiple_of(r * QB, QB)
        thr = _key_to_f32(lo_ref[r, 0:1, :])

        def count(pred):
            def body(c, part):
                off = pl.multiple_of(c * QB, QB)
                blk = sc_ref[pl.ds(off, QB), pl.ds(roff, QB)]
                return part + jnp.where(pred(blk, off + key_i), 1.0, 0.0)
            return col_sum(lax.fori_loop(0, r + 1, body, jnp.zeros((QB, QB), f32)))

        finite = thr > -jnp.inf
        tied = jnp.logical_and(count(lambda blk, pos: blk >= thr) != kf, finite)
        cut0 = jnp.where(finite, s_len, -1)

        def break_ties(_):
            need = kf - count(lambda blk, pos: blk > thr)

            def bisect_idx(_, lh):
                lo_j, hi_j = lh
                mid = (lo_j + hi_j) >> 1
                ok = count(lambda blk, pos: jnp.logical_and(blk == thr, pos <= mid)) >= need
                return jnp.where(ok, lo_j, mid), jnp.where(ok, mid, hi_j)

            n_it = int(np.ceil(np.log2(s_len + 1)))
            _, hi_j = lax.fori_loop(0, n_it, bisect_idx, (jnp.full((1, QB), -1, jnp.int32),
                                                          jnp.full((1, QB), s_len - 1, jnp.int32)))
            return jnp.where(tied, hi_j, cut0)

        cut = lax.cond(jnp.max(tied.astype(jnp.int32)) > 0, break_ties, lambda _: cut0, 0)

        def write_mask(c, carry2):
            off = pl.multiple_of(c * QB, QB)
            blk = sc_ref[pl.ds(off, QB), pl.ds(roff, QB)]
            sel = jnp.logical_or(blk > thr, jnp.logical_and(blk == thr, off + key_i <= cut))
            sc_ref[pl.ds(off, QB), pl.ds(roff, QB)] = jnp.where(sel, 0.0, NEG_BIG)
            return carry2

        per_g = rg // QB
        lax.fori_loop(0, per_g * (r // per_g + 1), write_mask, 0)
        return carry

    lax.fori_loop(0, n_rb, finalize, 0)

    def attend(g, carry):
        goff = pl.multiple_of(g * rg, rg)
        n_sup = g + 1

        def logits(c):
            off = pl.multiple_of(c * sup, sup)
            bias = sc_ref[pl.ds(off, sup), pl.ds(goff, rg)]
            out = []
            for hp in range(ATT_HEADS // 2):
                k_pair = k_ref[0, pl.ds(off, sup), hp * 2 * HEAD_DIM:(hp + 1) * 2 * HEAD_DIM]
                s2 = _dot(k_pair, qt_ref[0, g, hp])
                out += [s2[:, :rg] + bias, s2[:, rg:] + bias]
            return out, off

        def key_step(c, carry2):
            m_run, acc = carry2
            s_all, off = logits(c)
            m_new, acc_new = [], []
            for h in range(ATT_HEADS):
                m_h = jnp.maximum(m_run[h], jnp.max(_fold8(s_all[h], jnp.maximum), axis=0, keepdims=True))
                alpha = jnp.exp2(m_run[h] - m_h)
                p = jnp.exp2(s_all[h] - m_h).astype(bf16)
                vt_h = jnp.concatenate([vt_ref[0, h * HEAD_DIM:(h + 1) * HEAD_DIM, pl.ds(off, sup)], ones_rows], axis=0)
                acc_new.append(alpha * acc[h] + _dot(vt_h, p))
                m_new.append(m_h)
            return tuple(m_new), tuple(acc_new)

        ones_rows = jnp.ones((16, sup), bf16)
        _, acc = lax.fori_loop(0, n_sup, key_step,
                               (tuple(jnp.full((1, rg), NEG_BIG, f32) for _ in range(ATT_HEADS)),
                                tuple(jnp.zeros((HEAD_DIM + 16, rg), f32) for _ in range(ATT_HEADS))))
        outs = []
        for h in range(ATT_HEADS):
            o = acc[h][:HEAD_DIM] / acc[h][HEAD_DIM:HEAD_DIM + 1]
            inv = lax.rsqrt(jnp.mean(o * o, axis=0, keepdims=True) + EPS)
            outs.append(o * inv * nw_ref[h * HEAD_DIM:(h + 1) * HEAD_DIM, :])
        o_ref[0, pl.ds(goff, rg), :] = jnp.concatenate(outs, axis=0).T.astype(bf16)
        return carry

    lax.fori_loop(0, n_rg, attend, 0)


def _dsa(qt, qit, k, vt, kidx, wt, nw):
    b, s, _ = k.shape
    n_rb = s // QB
    n_sel = min(TOPK, s // 4)
    rg = qt.shape[-1] // 2
    s_pad = -(-s // rg) * rg
    kernel = functools.partial(_dsa_kernel, n_sel=n_sel)

    def per_batch(a):
        return pl.BlockSpec((1,) + a.shape[1:], lambda i: (i,) + (0,) * (a.ndim - 1))

    nw_col = nw.reshape(ATT_WIDTH, 1)
    return pl.pallas_call(
        kernel,
        out_shape=jax.ShapeDtypeStruct((b, s, ATT_WIDTH), bf16),
        grid=(b,),
        in_specs=[per_batch(qit), per_batch(kidx), per_batch(wt), per_batch(qt), per_batch(k), per_batch(vt),
                  pl.BlockSpec((ATT_WIDTH, 1), lambda i: (0, 0))],
        out_specs=pl.BlockSpec((1, s, ATT_WIDTH), lambda i: (i, 0, 0)),
        scratch_shapes=[pltpu.VMEM((s_pad, s), f32),
                        pltpu.VMEM((n_rb, 8, QB), jnp.int32),
                        pltpu.VMEM((n_rb, 8, QB), jnp.int32)],
        compiler_params=pltpu.CompilerParams(dimension_semantics=("arbitrary",),
                                             vmem_limit_bytes=VMEM_LIMIT),
        name="dsa_attention",
    )(qit, kidx, wt, qt, k, vt, nw_col)


ML_STATE_ROWS = ML_DIM + 16


def _mlstm_kernel(qm_ref, km_ref, vm_ref, om_ref, gt_ref, nw_ref, hm_ref, c_ref, *, chunk):
    s_len = qm_ref.shape[1]
    n_chunks = s_len // chunk
    s_i = lax.broadcasted_iota(jnp.int32, (chunk, chunk), 0)
    t_i = lax.broadcasted_iota(jnp.int32, (chunk, chunk), 1)
    causal = s_i <= t_i
    tri_t = jnp.where(causal, 1.0, 0.0).astype(f32)
    first_row = lax.broadcasted_iota(jnp.int32, (ML_STATE_ROWS - ML_DIM, chunk), 0) == 0
    c_ref[...] = jnp.zeros(c_ref.shape, f32)

    def body(c, m_state):
        off = pl.multiple_of(c * chunk, chunk)
        g_row = gt_ref[0, :, pl.ds(off, chunk)]
        cum_row = _dot(g_row, tri_t, precision=lax.Precision.HIGHEST)
        g_src = g_row[:ML_HEADS] - cum_row[ML_HEADS:]
        g_src_col = jnp.concatenate([g_src, jnp.zeros((LANE - ML_HEADS, chunk), f32)], axis=0).T
        m_next = []
        for h in range(ML_HEADS):
            ig_row = g_row[h:h + 1, :]
            b_row = cum_row[ML_HEADS + h:ML_HEADS + h + 1, :]
            m_s = m_state[h]
            total = b_row[:, chunk - 1:chunk]
            hs = slice(h * ML_DIM, (h + 1) * ML_DIM)
            q_t = qm_ref[0, pl.ds(off, chunk), hs].astype(f32).T.astype(bf16)
            k_h = km_ref[0, pl.ds(off, chunk), hs]
            v_tf = vm_ref[0, pl.ds(off, chunk), hs].astype(f32).T

            d_t = jnp.where(causal, g_src_col[:, h:h + 1] + b_row, -jnp.inf)
            inter = b_row + m_s
            m_row = jnp.maximum(inter, jnp.max(d_t, axis=0, keepdims=True))
            w_inter = jnp.exp(inter - m_row)
            s_t = _dot(k_h, q_t) * jnp.exp(d_t - m_row)
            c_ext = c_ref[h]
            qc = _dot(c_ext.astype(bf16), q_t)
            num = _dot(v_tf.astype(bf16), s_t.astype(bf16)) + w_inter * qc[:ML_DIM]
            den = jnp.sum(s_t, axis=0, keepdims=True) + w_inter * qc[ML_DIM:ML_DIM + 1]
            hh = num / jnp.maximum(jnp.abs(den), jnp.exp(-m_row))

            w_log = total - b_row + ig_row
            m_new = jnp.maximum(total + m_s, jnp.max(w_log, axis=1, keepdims=True))
            decay = jnp.exp(total + m_s - m_new)
            w_state = jnp.exp(w_log - m_new)
            lhs = jnp.concatenate([(v_tf * w_state).astype(bf16),
                                   jnp.where(first_row, w_state, 0.0).astype(bf16)], axis=0)
            c_ref[h] = decay * c_ext + _dot(lhs, k_h)
            m_next.append(m_new)

            inv = lax.rsqrt(jnp.mean(hh * hh, axis=0, keepdims=True) + EPS)
            gate = jax.nn.sigmoid(om_ref[0, pl.ds(off, chunk), hs].T)
            out_t = gate * hh * inv * nw_ref[hs, :]
            hm_ref[0, pl.ds(off, chunk), hs] = out_t.T.astype(bf16)
        return tuple(m_next)

    unroll = next(u for u in (4, 2, 1) if n_chunks % u == 0)
    lax.fori_loop(0, n_chunks, body, tuple(jnp.zeros((1, 1), f32) for _ in range(ML_HEADS)), unroll=unroll)


def _mlstm(qm, km, vm, om, gt, nw):
    b, s, _ = qm.shape
    chunk = min(ML_CHUNK, s)
    kernel = functools.partial(_mlstm_kernel, chunk=chunk)
    tok = pl.BlockSpec((1, s, ML_WIDTH), lambda i: (i, 0, 0))
    return pl.pallas_call(
        kernel,
        out_shape=jax.ShapeDtypeStruct((b, s, ML_WIDTH), bf16),
        grid=(b,),
        in_specs=[tok, tok, tok, tok,
                  pl.BlockSpec((1, 2 * ML_HEADS, s), lambda i: (i, 0, 0)),
                  pl.BlockSpec((ML_WIDTH, 1), lambda i: (0, 0))],
        out_specs=tok,
        scratch_shapes=[pltpu.VMEM((ML_HEADS, ML_STATE_ROWS, ML_DIM), f32)],
        compiler_params=pltpu.CompilerParams(dimension_semantics=("arbitrary",),
                                             vmem_limit_bytes=VMEM_LIMIT),
        name="mlstm",
    )(qm, km, vm, om, gt, nw.reshape(ML_WIDTH, 1))


def _out_ffn_kernel(att_ref, hm_ref, x_ref, mod_ref, wout_ref, npost_ref, fpre_ref, wgu_ref, wdn_ref, fpost_ref,
                    o_ref, *, bounds):
    hid = wdn_ref.shape[0]
    g_m = mod_ref[0, 2:3, :]
    sh_f = mod_ref[0, 3:4, :]
    sc_f = mod_ref[0, 4:5, :]
    g_f = mod_ref[0, 5:6, :]
    y = _dot(att_ref[0], wout_ref[0:ATT_WIDTH, :]) + _dot(hm_ref[0], wout_ref[ATT_WIDTH:, :])
    x1 = x_ref[0] + g_m * _rms(y, npost_ref[...])
    h2 = (_rms(x1, fpre_ref[...]) * (1.0 + sc_f) + sh_f).astype(bf16)
    y2 = jnp.zeros(x1.shape, f32)
    for c0, c1 in zip(bounds[:-1], bounds[1:]):
        g = _dot(h2, wgu_ref[:, c0:c1])
        u = _dot(h2, wgu_ref[:, hid + c0:hid + c1])
        a = (g * jax.nn.sigmoid(g) * u).astype(bf16)
        y2 = y2 + _dot(a, wdn_ref[c0:c1, :])
    o_ref[0] = x1 + g_f * _rms(y2, fpost_ref[...])


def _out_ffn(att, hm, x, mod_l, wout, npost, fpre, wgu, wdn, fpost, tm):
    b, s, d = x.shape
    hid = wdn.shape[0]
    mxu_tile = 256
    cut = -(-(hid // 2) // mxu_tile) * mxu_tile
    bounds = (0, cut, hid) if 0 < cut < hid else (0, hid)

    def tok(w):
        return pl.BlockSpec((1, tm, w), lambda i, j: (i, j, 0))

    def const(a):
        return pl.BlockSpec(a.shape, lambda i, j: (0,) * a.ndim, pipeline_mode=pl.Buffered(1))

    kernel = functools.partial(_out_ffn_kernel, bounds=bounds)
    return pl.pallas_call(
        kernel,
        out_shape=jax.ShapeDtypeStruct((b, s, d), f32),
        grid=(b, s // tm),
        in_specs=[tok(ATT_WIDTH), tok(ML_WIDTH), tok(d), pl.BlockSpec((1, 6, d), lambda i, j: (i, 0, 0)),
                  const(wout), const(npost), const(fpre), const(wgu), const(wdn), const(fpost)],
        out_specs=tok(d),
        compiler_params=pltpu.CompilerParams(dimension_semantics=("arbitrary", "arbitrary"),
                                             vmem_limit_bytes=VMEM_LIMIT),
        name="out_ffn",
    )(att, hm, x, mod_l, wout, npost, fpre, wgu, wdn, fpost)


def _arrange_w_in(w):
    d = w.shape[0]
    pts = [int(p) for p in np.cumsum(IN_SPLITS)[:-1]]
    cq, k_att, v_att, k_idx, w_idx, q_m, k_m, v_m, o_m, i_m, f_m = jnp.split(w, pts, axis=1)
    misc = jnp.concatenate([k_idx, w_idx, i_m, f_m, jnp.zeros((d, LANE - MISC_END), w.dtype)], axis=1)
    return jnp.concatenate([cq, k_att, v_att, misc, q_m, k_m, v_m, o_m], axis=1).astype(bf16)


def kernel(x, c, positions, w_mod, b_mod, mix_norm_pre, mix_norm_post, w_in, q_latent_norm, w_q_up, w_qidx_up,
           conv_w, conv_b, b_igate, b_fgate, attn_out_norm, mlstm_out_norm, w_out, ffn_norm_pre, ffn_norm_post,
           w_gate_up, w_down):
    depth = w_mod.shape[0]
    b, s, d = x.shape
    tm = min(512, s)
    mod = _modulation(c, w_mod, b_mod).reshape(depth, b, 6, d)
    cos, sin = _rope_tables(positions)

    def row(a):
        return a.reshape(1, -1)

    for l in range(depth):
        win = _arrange_w_in(w_in[l])
        wq = jnp.concatenate([w_q_up[l], w_qidx_up[l]], axis=1).astype(bf16)
        gb = jnp.zeros((1, LANE), f32).at[0, MISC_I:MISC_F].set(b_igate[l]).at[0, MISC_F:MISC_END].set(b_fgate[l])
        qt, qit, k, vt, kidx, wt, gt, qm, km, vm, om = _in_proj(
            x, mod[l], row(mix_norm_pre[l]), win, row(q_latent_norm[l]), wq, conv_w[l], row(conv_b[l]), gb,
            cos, sin, tm)
        att = _dsa(qt, qit, k, vt, kidx, wt, attn_out_norm[l])
        hm = _mlstm(qm, km, vm, om, gt, mlstm_out_norm[l])
        x = _out_ffn(att, hm, x, mod[l], w_out[l].astype(bf16), row(mix_norm_post[l]), row(ffn_norm_pre[l]),
                     w_gate_up[l].astype(bf16), w_down[l].astype(bf16), row(ffn_norm_post[l]), tm)
    return x
```

```python
import functools

import numpy as np
import jax
import jax.numpy as jnp
from jax import lax
from jax.experimental import pallas as pl
from jax.experimental.pallas import tpu as pltpu

EPS = 1e-6
ROPE_THETA = 10000.0
ATT_HEADS = 8
HEAD_DIM = 64
ATT_WIDTH = ATT_HEADS * HEAD_DIM
Q_RANK = 256
IDX_HEADS = 8
ML_HEADS = 4
ML_DIM = 128
ML_WIDTH = ML_HEADS * ML_DIM
CONV_W = 4
TOPK = 256
QB = 128
ATT_HEADS_PER_LOOP = 8
ATT_KEY_SPLIT = 1
FFN_ROW_PIECES = 1
LANE = 128
ML_CHUNK = 256
IN_SPLITS = (Q_RANK, ATT_WIDTH, ATT_WIDTH, HEAD_DIM, IDX_HEADS,
             ML_WIDTH, ML_WIDTH, ML_WIDTH, ML_WIDTH, ML_HEADS, ML_HEADS)

OFF_CQ = 0
OFF_K = OFF_CQ + Q_RANK
OFF_V = OFF_K + ATT_WIDTH
OFF_MISC = OFF_V + ATT_WIDTH
OFF_QKM = OFF_MISC + LANE
OFF_VM = OFF_QKM + 2 * ML_WIDTH
OFF_OM = OFF_VM + ML_WIDTH
IN_COLS_PADDED = OFF_OM + ML_WIDTH
MISC_W = HEAD_DIM
MISC_I = MISC_W + IDX_HEADS
MISC_F = MISC_I + ML_HEADS
MISC_END = MISC_F + ML_HEADS

VMEM_LIMIT = 56 * 1024 * 1024
NEG_BIG = -1e30
LOG2E = float(np.log2(np.e))
KEY_POS_INF = 0x7F800000
KEY_NEG_INF = (0xFF800000 - (1 << 32)) ^ 0x7FFFFFFF

bf16 = jnp.bfloat16
f32 = jnp.float32


def _rms(x, w):
    return x * lax.rsqrt(jnp.mean(x * x, axis=-1, keepdims=True) + EPS) * w


def _dot(a, b, precision=None):
    return jnp.dot(a, b, preferred_element_type=f32, precision=precision)


def _rope(x, cos, sin_signed):
    n = x.shape[1]
    lane = lax.broadcasted_iota(jnp.int32, x.shape, 1)
    low = (lane & (HEAD_DIM // 2)) == 0
    rot = jnp.where(low, pltpu.roll(x, n - HEAD_DIM // 2, 1), pltpu.roll(x, HEAD_DIM // 2, 1))
    return x * cos + rot * sin_signed


def _mod_kernel(c_ref, w_ref, b_ref, o_ref):
    c = c_ref[...]
    c_act = c * jax.nn.sigmoid(c)
    o_ref[0] = _dot(c_act, w_ref[0], precision=lax.Precision.HIGHEST) + b_ref[0]


def _modulation(c, w_mod, b_mod):
    depth, d, d6 = w_mod.shape
    b = c.shape[0]
    nj = d6 // d
    return pl.pallas_call(
        _mod_kernel,
        out_shape=jax.ShapeDtypeStruct((depth, b, d6), f32),
        grid=(depth, nj),
        in_specs=[pl.BlockSpec((b, d), lambda l, j: (0, 0)),
                  pl.BlockSpec((1, d, d), lambda l, j: (l, 0, j)),
                  pl.BlockSpec((1, 1, d), lambda l, j: (l, 0, j))],
        out_specs=pl.BlockSpec((1, b, d), lambda l, j: (l, 0, j)),
        name="modulation",
    )(c, w_mod, b_mod.reshape(depth, 1, d6))


def _rope_table_kernel(pos_ref, invf_ref, sign_ref, cos_ref, sin_ref):
    ang = pos_ref[0].astype(f32) * invf_ref[...]
    cos_ref[0] = jnp.cos(ang)
    sin_ref[0] = jnp.sin(ang) * sign_ref[...]


def _rope_tables(positions):
    b, s = positions.shape
    ts = min(s, 512)
    inv_freq = ROPE_THETA ** (-jnp.arange(0, HEAD_DIM, 2, dtype=f32) / HEAD_DIM)
    invf = jnp.tile(inv_freq, LANE // (HEAD_DIM // 2)).reshape(1, LANE)
    sign = np.where((np.arange(LANE) % HEAD_DIM) < HEAD_DIM // 2, -1.0, 1.0).astype(np.float32).reshape(1, LANE)
    tab = jax.ShapeDtypeStruct((b, s, LANE), f32)
    return pl.pallas_call(
        _rope_table_kernel,
        out_shape=(tab, tab),
        grid=(b, s // ts),
        in_specs=[pl.BlockSpec((1, ts, 1), lambda i, j: (i, j, 0)),
                  pl.BlockSpec((1, LANE), lambda i, j: (0, 0)),
                  pl.BlockSpec((1, LANE), lambda i, j: (0, 0))],
        out_specs=(pl.BlockSpec((1, ts, LANE), lambda i, j: (i, j, 0)),
                   pl.BlockSpec((1, ts, LANE), lambda i, j: (i, j, 0))),
        name="rope_tables",
    )(positions.reshape(b, s, 1), invf, jnp.asarray(sign))


def _in_proj_kernel(x_ref, mod_ref, nw_ref, win_ref, qln_ref, wq_ref, cw_ref, cb_ref, gb_ref, cos_ref, sin_ref,
                    qt_ref, qit_ref, k_ref, vt_ref, kidx_ref, wt_ref, gt_ref, qm_ref, km_ref, vm_ref, om_ref,
                    buf_ref):
    tm = x_ref.shape[1]

    @pl.when(pl.program_id(1) == 0)
    def _():
        buf_ref[0:8, :] = jnp.zeros((8, 2 * ML_WIDTH), f32)

    x = x_ref[0]
    sh = mod_ref[0, 0:1, :]
    sc = mod_ref[0, 1:2, :]
    hb = (_rms(x, nw_ref[...]) * (1.0 + sc) + sh).astype(bf16)

    cos = cos_ref[0]
    sin = sin_ref[0]
    reps = ATT_WIDTH // LANE
    cos_w = jnp.concatenate([cos] * reps, axis=1)
    sin_w = jnp.concatenate([sin] * reps, axis=1)

    cq = _dot(hb, win_ref[:, OFF_CQ:OFF_CQ + Q_RANK])
    cqn = _rms(cq, qln_ref[...]).astype(bf16)
    qq = _dot(cqn, wq_ref[...])
    q_t = (_rope(qq[:, :ATT_WIDTH], cos_w, sin_w) * (HEAD_DIM ** -0.5 * LOG2E)).T.astype(bf16)
    qi_t = _rope(qq[:, ATT_WIDTH:], cos_w, sin_w).T.astype(bf16)
    qt_ref[...] = jnp.zeros(qt_ref.shape, bf16)
    for h in range(ATT_HEADS):
        d0 = (h % 2) * HEAD_DIM
        qt_ref[0, 0, h // 2, d0:d0 + HEAD_DIM, (h % 2) * tm:(h % 2 + 1) * tm] = q_t[h * HEAD_DIM:(h + 1) * HEAD_DIM, :]
    for h in range(IDX_HEADS):
        qit_ref[0, 0, :, h * tm:(h + 1) * tm] = qi_t[h * HEAD_DIM:(h + 1) * HEAD_DIM, :]
    k_ref[0] = _rope(_dot(hb, win_ref[:, OFF_K:OFF_K + ATT_WIDTH]), cos_w, sin_w).astype(bf16)
    vt_ref[0] = _dot(hb, win_ref[:, OFF_V:OFF_V + ATT_WIDTH]).T.astype(bf16)

    m = _dot(hb, win_ref[:, OFF_MISC:OFF_MISC + LANE])
    lane = lax.broadcasted_iota(jnp.int32, m.shape, 1)
    roped = _rope(m, cos, sin)
    biased = m + gb_ref[...]
    logsig = jnp.minimum(biased, 0.0) - jnp.log1p(jnp.exp(-jnp.abs(biased)))
    w_scale = (IDX_HEADS * HEAD_DIM) ** -0.5
    misc = jnp.where(lane < MISC_W, roped,
                     jnp.where(lane < MISC_I, m * w_scale,
                               jnp.where(lane < MISC_F, biased,
                                         jnp.where(lane < MISC_END, logsig, 0.0))))
    misc_t = misc.T
    kidx_ref[0] = misc[:, 0:HEAD_DIM].astype(bf16)
    wt_ref[0] = misc_t[MISC_W:MISC_I, :]
    gt_ref[0] = misc_t[MISC_I:MISC_END, :]

    pre = _dot(hb, win_ref[:, OFF_QKM:OFF_QKM + 2 * ML_WIDTH])
    buf_ref[8:8 + tm, :] = pre
    acc = jnp.broadcast_to(cb_ref[...], pre.shape)
    for t in range(CONV_W):
        acc = acc + cw_ref[t:t + 1, :] * buf_ref[8 - (CONV_W - 1) + t:8 - (CONV_W - 1) + t + tm, :]
    buf_ref[0:8, :] = buf_ref[tm:tm + 8, :]
    qk = acc * jax.nn.sigmoid(acc)
    qm_ref[0] = (qk[:, :ML_WIDTH] * (ML_DIM ** -0.5)).astype(bf16)
    km_ref[0] = qk[:, ML_WIDTH:].astype(bf16)
    vm_ref[0] = _dot(hb, win_ref[:, OFF_VM:OFF_VM + ML_WIDTH]).astype(bf16)
    om_ref[0] = _dot(hb, win_ref[:, OFF_OM:OFF_OM + ML_WIDTH])


def _in_proj(x, mod_l, nw, win, qln, wq, cw, cb, gb, cos, sin, tm):
    b, s, d = x.shape
    nt = s // tm

    def tok(w):
        return pl.BlockSpec((1, tm, w), lambda i, j: (i, j, 0))

    def tok_t(r):
        return pl.BlockSpec((1, r, tm), lambda i, j: (i, 0, j))

    def full(a):
        return pl.BlockSpec(a.shape, lambda i, j: (0,) * a.ndim)

    sd = jax.ShapeDtypeStruct
    qt_shape = (ATT_HEADS // 2, 2 * HEAD_DIM, 2 * tm)
    qit_shape = (HEAD_DIM, IDX_HEADS * tm)
    out_shape = (sd((b, nt) + qt_shape, bf16), sd((b, nt) + qit_shape, bf16), sd((b, s, ATT_WIDTH), bf16),
                 sd((b, ATT_WIDTH, s), bf16), sd((b, s, HEAD_DIM), bf16), sd((b, IDX_HEADS, s), f32),
                 sd((b, 2 * ML_HEADS, s), f32), sd((b, s, ML_WIDTH), bf16),
                 sd((b, s, ML_WIDTH), bf16), sd((b, s, ML_WIDTH), bf16), sd((b, s, ML_WIDTH), f32))
    out_specs = (pl.BlockSpec((1, 1) + qt_shape, lambda i, j: (i, j, 0, 0, 0)),
                 pl.BlockSpec((1, 1) + qit_shape, lambda i, j: (i, j, 0, 0)),
                 tok(ATT_WIDTH), tok_t(ATT_WIDTH), tok(HEAD_DIM), tok_t(IDX_HEADS),
                 tok_t(2 * ML_HEADS), tok(ML_WIDTH), tok(ML_WIDTH), tok(ML_WIDTH), tok(ML_WIDTH))
    return pl.pallas_call(
        _in_proj_kernel,
        out_shape=out_shape,
        grid=(b, nt),
        in_specs=[tok(d), pl.BlockSpec((1, 6, d), lambda i, j: (i, 0, 0)), full(nw), full(win), full(qln), full(wq),
                  full(cw), full(cb), full(gb), tok(LANE), tok(LANE)],
        out_specs=out_specs,
        scratch_shapes=[pltpu.VMEM((tm + 8, 2 * ML_WIDTH), f32)],
        compiler_params=pltpu.CompilerParams(dimension_semantics=("arbitrary", "arbitrary"),
                                             vmem_limit_bytes=VMEM_LIMIT),
        name="in_proj",
    )(x, mod_l, nw, win, qln, wq, cw, cb, gb, cos, sin)


def _key_to_f32(key):
    bits = jnp.where(key >= 0, key, key ^ jnp.int32(0x7FFFFFFF))
    return lax.bitcast_convert_type(bits, f32)


def _fold8(x, op):
    parts = [x[i * 8:(i + 1) * 8] for i in range(x.shape[0] // 8)]
    n_acc = min(4, len(parts))
    accs = parts[:n_acc]
    for i, part in enumerate(parts[n_acc:]):
        accs[i % n_acc] = op(accs[i % n_acc], part)
    while len(accs) > 1:
        accs = [op(accs[i], accs[i + 1]) for i in range(0, len(accs), 2)]
    return accs[0]


def _dsa_kernel(qit_ref, kidx_ref, wt_ref, qt_ref, k_ref, vt_ref, nw_ref, o_ref, sc_ref, lo_ref, hi_ref, *, n_sel):
    s_len = k_ref.shape[1]
    n_rb = s_len // QB
    rg = qt_ref.shape[-1] // 2
    n_rg = s_len // rg
    sup = rg
    kf = float(n_sel)
    key_i = lax.broadcasted_iota(jnp.int32, (QB, QB), 0)
    row_i = lax.broadcasted_iota(jnp.int32, (QB, QB), 1)
    key_g = lax.broadcasted_iota(jnp.int32, (QB, rg), 0)
    row_g = lax.broadcasted_iota(jnp.int32, (QB, rg), 1)

    def col_sum(x):
        return jnp.sum(x, axis=0, keepdims=True)

    def score_rows(g, carry):
        goff = pl.multiple_of(g * rg, rg)
        w = wt_ref[0, :, pl.ds(goff, rg)]

        def score_keys(c, carry2):
            for p in range(sup // QB):
                off = pl.multiple_of(c * sup + p * QB, QB)
                lg = _dot(kidx_ref[0, pl.ds(off, QB), :], qit_ref[0, g])
                acc = jnp.zeros((QB, rg), f32)
                for h in range(IDX_HEADS):
                    acc = acc + w[h:h + 1, :] * jnp.maximum(lg[:, h * rg:(h + 1) * rg], 0.0)
                sc_ref[pl.ds(off, QB), pl.ds(goff, rg)] = jnp.where(off + key_g <= goff + row_g, acc, -jnp.inf)
            return carry2

        lax.fori_loop(0, g + 1, score_keys, 0)
        return carry

    lax.fori_loop(0, n_rg, score_rows, 0)

    first = n_sel // QB
    ones_lhs = jnp.ones((16, s_len), bf16)

    def floor_avg(a, b):
        return (a & b) + ((a ^ b) >> 1)

    if (rg // QB) % 2 == 0:
        units = [(first, 1)] if first % 2 else []
        units += [(r, 2) for r in range(first + first % 2, n_rb - 1, 2)]
        if (n_rb - first - first % 2) % 2:
            units.append((n_rb - 1, 1))
    else:
        units = [(r, 1) for r in range(first, n_rb)]

    def search_pass(_, carry):
        for r, width in units:
            blocks = range(r, r + width)
            lo = jnp.concatenate([lo_ref[b, 0:1, :] for b in blocks], axis=1)
            hi = jnp.concatenate([hi_ref[b, 0:1, :] for b in blocks], axis=1)
            mid = floor_avg(lo, hi)
            n_keys = (r + width) * QB
            hit = jnp.where(sc_ref[0:n_keys, r * QB:(r + width) * QB] >= _key_to_f32(mid), 1.0, 0.0).astype(bf16)
            ge = _dot(ones_lhs[:, :n_keys], hit)[0:1] >= kf
            new_lo = jnp.where(ge, mid, lo)
            new_hi = jnp.where(ge, hi, mid)
            for i, b in enumerate(blocks):
                lo_ref[b, 0:1, :] = new_lo[:, i * QB:(i + 1) * QB]
                hi_ref[b, 0:1, :] = new_hi[:, i * QB:(i + 1) * QB]
        return carry

    lo_ref[...] = jnp.full(lo_ref.shape, KEY_NEG_INF, jnp.int32)
    hi_ref[...] = jnp.full(hi_ref.shape, KEY_POS_INF, jnp.int32)
    lax.fori_loop(0, 32, search_pass, 0)

    def finalize(r, carry):
        roff = pl.multiple_of(r * QB, QB)
        thr = _key_to_f32(lo_ref[r, 0:1, :])

        def count(pred):
            def body(c, part):
                off = pl.multiple_of(c * QB, QB)
                blk = sc_ref[pl.ds(off, QB), pl.ds(roff, QB)]
                return part + jnp.where(pred(blk, off + key_i), 1.0, 0.0)
            return col_sum(lax.fori_loop(0, r + 1, body, jnp.zeros((QB, QB), f32)))

        finite = thr > -jnp.inf
        tied = jnp.logical_and(count(lambda blk, pos: blk >= thr) != kf, finite)
        cut0 = jnp.where(finite, s_len, -1)

        def break_ties(_):
            need = kf - count(lambda blk, pos: blk > thr)

            def bisect_idx(_, lh):
                lo_j, hi_j = lh
                mid = (lo_j + hi_j) >> 1
                ok = count(lambda blk, pos: jnp.logical_and(blk == thr, pos <= mid)) >= need
                return jnp.where(ok, lo_j, mid), jnp.where(ok, mid, hi_j)

            n_it = int(np.ceil(np.log2(s_len + 1)))
            _, hi_j = lax.fori_loop(0, n_it, bisect_idx, (jnp.full((1, QB), -1, jnp.int32),
                                                          jnp.full((1, QB), s_len - 1, jnp.int32)))
            return jnp.where(tied, hi_j, cut0)

        cut = lax.cond(jnp.max(tied.astype(jnp.int32)) > 0, break_ties, lambda _: cut0, 0)

        def write_mask(c, carry2):
            off = pl.multiple_of(c * QB, QB)
            blk = sc_ref[pl.ds(off, QB), pl.ds(roff, QB)]
            sel = jnp.logical_or(blk > thr, jnp.logical_and(blk == thr, off + key_i <= cut))
            sc_ref[pl.ds(off, QB), pl.ds(roff, QB)] = jnp.where(sel, 0.0, NEG_BIG)
            return carry2

        lax.fori_loop(0, per_g * (r // per_g + 1), write_mask, 0)
        return carry

    per_g = rg // QB

    any_tied = jnp.zeros((1, QB), jnp.int32)
    for r, width in units:
        n_keys = (r + width) * QB
        thr = jnp.concatenate([_key_to_f32(lo_ref[b, 0:1, :]) for b in range(r, r + width)], axis=1)
        hit = jnp.where(sc_ref[0:n_keys, r * QB:(r + width) * QB] >= thr, 1.0, 0.0).astype(bf16)
        over = (_dot(ones_lhs[:, :n_keys], hit)[0:1] != kf).astype(jnp.int32)
        for i in range(width):
            any_tied = jnp.maximum(any_tied, over[:, i * QB:(i + 1) * QB])

    def with_ties(_):
        lax.fori_loop(0, n_rb, finalize, 0)
        return 0

    def without_ties(_):
        for r in range(n_rb):
            thr = _key_to_f32(lo_ref[r, 0:1, :])
            thr = jnp.maximum(thr, float(jnp.finfo(f32).min))
            for c in range(per_g * (r // per_g + 1)):
                tile = (slice(c * QB, (c + 1) * QB), slice(r * QB, (r + 1) * QB))
                sc_ref[tile] = jnp.where(sc_ref[tile] >= thr, 0.0, NEG_BIG)
        return 0

    lax.cond(jnp.max(any_tied) > 0, with_ties, without_ties, 0)

    def attend(g, carry, sup=sup // ATT_KEY_SPLIT):
        goff = pl.multiple_of(g * rg, rg)
        n_sup = (g + 1) * ATT_KEY_SPLIT

        ones_rows = jnp.ones((16, sup), bf16)
        outs = []
        for h0 in range(0, ATT_HEADS, ATT_HEADS_PER_LOOP):
            heads = range(h0, h0 + ATT_HEADS_PER_LOOP)

            def key_step(c, carry2, heads=heads):
                m_run, acc = carry2
                off = pl.multiple_of(c * sup, sup)
                bias = sc_ref[pl.ds(off, sup), pl.ds(goff, rg)]
                s_all = []
                for hp in range(heads[0] // 2, heads[-1] // 2 + 1):
                    k_pair = k_ref[0, pl.ds(off, sup), hp * 2 * HEAD_DIM:(hp + 1) * 2 * HEAD_DIM]
                    s2 = _dot(k_pair, qt_ref[0, g, hp])
                    s_all += [s2[:, :rg] + bias, s2[:, rg:] + bias]
                m_new, acc_new = [], []
                for i, h in enumerate(heads):
                    m_h = jnp.maximum(m_run[i], jnp.max(_fold8(s_all[i], jnp.maximum), axis=0, keepdims=True))
                    alpha = jnp.exp2(m_run[i] - m_h)
                    p = jnp.exp2(s_all[i] - m_h).astype(bf16)
                    vt_h = jnp.concatenate([vt_ref[0, h * HEAD_DIM:(h + 1) * HEAD_DIM, pl.ds(off, sup)], ones_rows],
                                           axis=0)
                    acc_new.append(alpha * acc[i] + _dot(vt_h, p))
                    m_new.append(m_h)
                return tuple(m_new), tuple(acc_new)

            _, acc = lax.fori_loop(0, n_sup, key_step,
                                   (tuple(jnp.full((1, rg), NEG_BIG, f32) for _ in heads),
                                    tuple(jnp.zeros((HEAD_DIM + 16, rg), f32) for _ in heads)))
            for i, h in enumerate(heads):
                o = acc[i][:HEAD_DIM] / acc[i][HEAD_DIM:HEAD_DIM + 1]
                inv = lax.rsqrt(jnp.mean(o * o, axis=0, keepdims=True) + EPS)
                outs.append(o * inv * nw_ref[h * HEAD_DIM:(h + 1) * HEAD_DIM, :])
        o_ref[0, pl.ds(goff, rg), :] = jnp.concatenate(outs, axis=0).T.astype(bf16)
        return carry

    lax.fori_loop(0, n_rg, attend, 0)


def _dsa(qt, qit, k, vt, kidx, wt, nw):
    b, s, _ = k.shape
    n_rb = s // QB
    n_sel = min(TOPK, s // 4)
    rg = qt.shape[-1] // 2
    s_pad = -(-s // rg) * rg
    kernel = functools.partial(_dsa_kernel, n_sel=n_sel)

    def per_batch(a):
        return pl.BlockSpec((1,) + a.shape[1:], lambda i: (i,) + (0,) * (a.ndim - 1))

    nw_col = nw.reshape(ATT_WIDTH, 1)
    return pl.pallas_call(
        kernel,
        out_shape=jax.ShapeDtypeStruct((b, s, ATT_WIDTH), bf16),
        grid=(b,),
        in_specs=[per_batch(qit), per_batch(kidx), per_batch(wt), per_batch(qt), per_batch(k), per_batch(vt),
                  pl.BlockSpec((ATT_WIDTH, 1), lambda i: (0, 0))],
        out_specs=pl.BlockSpec((1, s, ATT_WIDTH), lambda i: (i, 0, 0)),
        scratch_shapes=[pltpu.VMEM((s_pad, s), f32),
                        pltpu.VMEM((n_rb, 8, QB), jnp.int32),
                        pltpu.VMEM((n_rb, 8, QB), jnp.int32)],
        compiler_params=pltpu.CompilerParams(dimension_semantics=("arbitrary",),
                                             vmem_limit_bytes=VMEM_LIMIT),
        name="dsa_attention",
    )(qit, kidx, wt, qt, k, vt, nw_col)


ML_STATE_ROWS = ML_DIM + 16


def _mlstm_kernel(qm_ref, km_ref, vm_ref, om_ref, gt_ref, nw_ref, hm_ref, c_ref, *, chunk):
    s_len = qm_ref.shape[1]
    n_chunks = s_len // chunk
    s_i = lax.broadcasted_iota(jnp.int32, (chunk, chunk), 0)
    t_i = lax.broadcasted_iota(jnp.int32, (chunk, chunk), 1)
    causal = s_i <= t_i
    tri_t = jnp.where(causal, 1.0, 0.0).astype(f32)
    first_row = lax.broadcasted_iota(jnp.int32, (ML_STATE_ROWS - ML_DIM, chunk), 0) == 0
    c_ref[...] = jnp.zeros(c_ref.shape, f32)

    def body(c, m_state):
        off = pl.multiple_of(c * chunk, chunk)
        g_row = gt_ref[0, :, pl.ds(off, chunk)]
        cum_row = _dot(g_row, tri_t, precision=lax.Precision.HIGHEST)
        g_src = g_row[:ML_HEADS] - cum_row[ML_HEADS:]
        g_src_col = jnp.concatenate([g_src, jnp.zeros((LANE - ML_HEADS, chunk), f32)], axis=0).T
        m_next = []
        for h in range(ML_HEADS):
            ig_row = g_row[h:h + 1, :]
            b_row = cum_row[ML_HEADS + h:ML_HEADS + h + 1, :]
            m_s = m_state[h]
            total = b_row[:, chunk - 1:chunk]
            hs = slice(h * ML_DIM, (h + 1) * ML_DIM)
            q_t = qm_ref[0, pl.ds(off, chunk), hs].astype(f32).T.astype(bf16)
            k_h = km_ref[0, pl.ds(off, chunk), hs]
            v_tf = vm_ref[0, pl.ds(off, chunk), hs].astype(f32).T

            d_t = jnp.where(causal, g_src_col[:, h:h + 1] + b_row, -jnp.inf)
            inter = b_row + m_s
            m_row = jnp.maximum(inter, jnp.max(d_t, axis=0, keepdims=True))
            w_inter = jnp.exp(inter - m_row)
            s_t = _dot(k_h, q_t) * jnp.exp(d_t - m_row)
            c_ext = c_ref[h]
            qc = _dot(c_ext.astype(bf16), q_t)
            num = _dot(v_tf.astype(bf16), s_t.astype(bf16)) + w_inter * qc[:ML_DIM]
            den = jnp.sum(s_t, axis=0, keepdims=True) + w_inter * qc[ML_DIM:ML_DIM + 1]
            hh = num / jnp.maximum(jnp.abs(den), jnp.exp(-m_row))

            w_log = total - b_row + ig_row
            m_new = jnp.maximum(total + m_s, jnp.max(w_log, axis=1, keepdims=True))
            decay = jnp.exp(total + m_s - m_new)
            w_state = jnp.exp(w_log - m_new)
            lhs = jnp.concatenate([(v_tf * w_state).astype(bf16),
                                   jnp.where(first_row, w_state, 0.0).astype(bf16)], axis=0)
            c_ref[h] = decay * c_ext + _dot(lhs, k_h)
            m_next.append(m_new)

            inv = lax.rsqrt(jnp.mean(hh * hh, axis=0, keepdims=True) + EPS)
            gate = jax.nn.sigmoid(om_ref[0, pl.ds(off, chunk), hs].T)
            out_t = gate * hh * inv * nw_ref[hs, :]
            hm_ref[0, pl.ds(off, chunk), hs] = out_t.T.astype(bf16)
        return tuple(m_next)

    unroll = next(u for u in (4, 2, 1) if n_chunks % u == 0)
    lax.fori_loop(0, n_chunks, body, tuple(jnp.zeros((1, 1), f32) for _ in range(ML_HEADS)), unroll=unroll)


def _mlstm(qm, km, vm, om, gt, nw):
    b, s, _ = qm.shape
    chunk = min(ML_CHUNK, s)
    kernel = functools.partial(_mlstm_kernel, chunk=chunk)
    tok = pl.BlockSpec((1, s, ML_WIDTH), lambda i: (i, 0, 0))
    return pl.pallas_call(
        kernel,
        out_shape=jax.ShapeDtypeStruct((b, s, ML_WIDTH), bf16),
        grid=(b,),
        in_specs=[tok, tok, tok, tok,
                  pl.BlockSpec((1, 2 * ML_HEADS, s), lambda i: (i, 0, 0)),
                  pl.BlockSpec((ML_WIDTH, 1), lambda i: (0, 0))],
        out_specs=tok,
        scratch_shapes=[pltpu.VMEM((ML_HEADS, ML_STATE_ROWS, ML_DIM), f32)],
        compiler_params=pltpu.CompilerParams(dimension_semantics=("arbitrary",),
                                             vmem_limit_bytes=VMEM_LIMIT),
        name="mlstm",
    )(qm, km, vm, om, gt, nw.reshape(ML_WIDTH, 1))


def _out_ffn_kernel(att_ref, hm_ref, x_ref, mod_ref, wout_ref, npost_ref, fpre_ref, wgu_ref, wdn_ref, fpost_ref,
                    o_ref, *, bounds):
    hid = wdn_ref.shape[0]
    g_m = mod_ref[0, 2:3, :]
    sh_f = mod_ref[0, 3:4, :]
    sc_f = mod_ref[0, 4:5, :]
    g_f = mod_ref[0, 5:6, :]
    tm = x_ref.shape[1]
    piece = tm // FFN_ROW_PIECES
    for r0 in range(0, tm, piece):
        rows = slice(r0, r0 + piece)
        y = _dot(att_ref[0, rows, :], wout_ref[0:ATT_WIDTH, :]) + _dot(hm_ref[0, rows, :], wout_ref[ATT_WIDTH:, :])
        x1 = x_ref[0, rows, :] + g_m * _rms(y, npost_ref[...])
        h2 = (_rms(x1, fpre_ref[...]) * (1.0 + sc_f) + sh_f).astype(bf16)
        y2 = jnp.zeros(x1.shape, f32)
        for c0, c1 in zip(bounds[:-1], bounds[1:]):
            g = _dot(h2, wgu_ref[:, c0:c1])
            u = _dot(h2, wgu_ref[:, hid + c0:hid + c1])
            a = (g * jax.nn.sigmoid(g) * u).astype(bf16)
            y2 = y2 + _dot(a, wdn_ref[c0:c1, :])
        o_ref[0, rows, :] = x1 + g_f * _rms(y2, fpost_ref[...])


def _out_ffn(att, hm, x, mod_l, wout, npost, fpre, wgu, wdn, fpost, tm):
    b, s, d = x.shape
    hid = wdn.shape[0]
    mxu_tile = 256
    cut = -(-(hid // 2) // mxu_tile) * mxu_tile
    bounds = (0, cut, hid) if 0 < cut < hid else (0, hid)

    def tok(w):
        return pl.BlockSpec((1, tm, w), lambda i, j: (i, j, 0))

    def const(a):
        return pl.BlockSpec(a.shape, lambda i, j: (0,) * a.ndim, pipeline_mode=pl.Buffered(1))

    kernel = functools.partial(_out_ffn_kernel, bounds=bounds)
    return pl.pallas_call(
        kernel,
        out_shape=jax.ShapeDtypeStruct((b, s, d), f32),
        grid=(b, s // tm),
        in_specs=[tok(ATT_WIDTH), tok(ML_WIDTH), tok(d), pl.BlockSpec((1, 6, d), lambda i, j: (i, 0, 0)),
                  const(wout), const(npost), const(fpre), const(wgu), const(wdn), const(fpost)],
        out_specs=tok(d),
        compiler_params=pltpu.CompilerParams(dimension_semantics=("arbitrary", "arbitrary"),
                                             vmem_limit_bytes=VMEM_LIMIT),
        name="out_ffn",
    )(att, hm, x, mod_l, wout, npost, fpre, wgu, wdn, fpost)


def _arrange_w_in(w):
    d = w.shape[0]
    pts = [int(p) for p in np.cumsum(IN_SPLITS)[:-1]]
    cq, k_att, v_att, k_idx, w_idx, q_m, k_m, v_m, o_m, i_m, f_m = jnp.split(w, pts, axis=1)
    misc = jnp.concatenate([k_idx, w_idx, i_m, f_m, jnp.zeros((d, LANE - MISC_END), w.dtype)], axis=1)
    return jnp.concatenate([cq, k_att, v_att, misc, q_m, k_m, v_m, o_m], axis=1).astype(bf16)


def kernel(x, c, positions, w_mod, b_mod, mix_norm_pre, mix_norm_post, w_in, q_latent_norm, w_q_up, w_qidx_up,
           conv_w, conv_b, b_igate, b_fgate, attn_out_norm, mlstm_out_norm, w_out, ffn_norm_pre, ffn_norm_post,
           w_gate_up, w_down):
    depth = w_mod.shape[0]
    b, s, d = x.shape
    tm = min(512, s)
    mod = _modulation(c, w_mod, b_mod).reshape(depth, b, 6, d)
    cos, sin = _rope_tables(positions)

    def row(a):
        return a.reshape(1, -1)

    for l in range(depth):
        win = _arrange_w_in(w_in[l])
        wq = jnp.concatenate([w_q_up[l], w_qidx_up[l]], axis=1).astype(bf16)
        gb = jnp.zeros((1, LANE), f32).at[0, MISC_I:MISC_F].set(b_igate[l]).at[0, MISC_F:MISC_END].set(b_fgate[l])
        qt, qit, k, vt, kidx, wt, gt, qm, km, vm, om = _in_proj(
            x, mod[l], row(mix_norm_pre[l]), win, row(q_latent_norm[l]), wq, conv_w[l], row(conv_b[l]), gb,
            cos, sin, tm)
        att = _dsa(qt, qit, k, vt, kidx, wt, attn_out_norm[l])
        hm = _mlstm(qm, km, vm, om, gt, mlstm_out_norm[l])
        x = _out_ffn(att, hm, x, mod[l], w_out[l].astype(bf16), row(mix_norm_post[l]), row(ffn_norm_pre[l]),
                     w_gate_up[l].astype(bf16), w_down[l].astype(bf16), row(ffn_norm_post[l]), tm)
    return x
```

```python
import functools

import numpy as np
import jax
import jax.numpy as jnp
from jax import lax
from jax.experimental import pallas as pl
from jax.experimental.pallas import tpu as pltpu

EPS = 1e-6
ROPE_THETA = 10000.0
ATT_HEADS = 8
HEAD_DIM = 64
ATT_WIDTH = ATT_HEADS * HEAD_DIM
Q_RANK = 256
IDX_HEADS = 8
ML_HEADS = 4
ML_DIM = 128
ML_WIDTH = ML_HEADS * ML_DIM
CONV_W = 4
TOPK = 256
QB = 128
ATT_HEADS_PER_LOOP = 8
ATT_KEY_SPLIT = 1
FFN_ROW_PIECES = 1
LANE = 128
ML_CHUNK = 256
IN_SPLITS = (Q_RANK, ATT_WIDTH, ATT_WIDTH, HEAD_DIM, IDX_HEADS,
             ML_WIDTH, ML_WIDTH, ML_WIDTH, ML_WIDTH, ML_HEADS, ML_HEADS)

OFF_CQ = 0
OFF_K = OFF_CQ + Q_RANK
OFF_V = OFF_K + ATT_WIDTH
OFF_MISC = OFF_V + ATT_WIDTH
OFF_QKM = OFF_MISC + LANE
OFF_VM = OFF_QKM + 2 * ML_WIDTH
OFF_OM = OFF_VM + ML_WIDTH
IN_COLS_PADDED = OFF_OM + ML_WIDTH
MISC_W = HEAD_DIM
MISC_I = MISC_W + IDX_HEADS
MISC_F = MISC_I + ML_HEADS
MISC_END = MISC_F + ML_HEADS

VMEM_LIMIT = 56 * 1024 * 1024
NEG_BIG = -1e30
LOG2E = float(np.log2(np.e))
KEY_POS_INF = 0x7F800000
KEY_NEG_INF = (0xFF800000 - (1 << 32)) ^ 0x7FFFFFFF

bf16 = jnp.bfloat16
f32 = jnp.float32


def _rms(x, w):
    return x * lax.rsqrt(jnp.mean(x * x, axis=-1, keepdims=True) + EPS) * w


def _dot(a, b, precision=None):
    return jnp.dot(a, b, preferred_element_type=f32, precision=precision)


def _rope(x, cos, sin_signed):
    n = x.shape[1]
    lane = lax.broadcasted_iota(jnp.int32, x.shape, 1)
    low = (lane & (HEAD_DIM // 2)) == 0
    rot = jnp.where(low, pltpu.roll(x, n - HEAD_DIM // 2, 1), pltpu.roll(x, HEAD_DIM // 2, 1))
    return x * cos + rot * sin_signed


def _mod_kernel(c_ref, w_ref, b_ref, o_ref):
    c = c_ref[...]
    c_act = c * jax.nn.sigmoid(c)
    o_ref[0] = _dot(c_act, w_ref[0], precision=lax.Precision.HIGHEST) + b_ref[0]


def _modulation(c, w_mod, b_mod):
    depth, d, d6 = w_mod.shape
    b = c.shape[0]
    nj = d6 // d
    return pl.pallas_call(
        _mod_kernel,
        out_shape=jax.ShapeDtypeStruct((depth, b, d6), f32),
        grid=(depth, nj),
        in_specs=[pl.BlockSpec((b, d), lambda l, j: (0, 0)),
                  pl.BlockSpec((1, d, d), lambda l, j: (l, 0, j)),
                  pl.BlockSpec((1, 1, d), lambda l, j: (l, 0, j))],
        out_specs=pl.BlockSpec((1, b, d), lambda l, j: (l, 0, j)),
        name="modulation",
    )(c, w_mod, b_mod.reshape(depth, 1, d6))


def _rope_table_kernel(pos_ref, invf_ref, sign_ref, cos_ref, sin_ref):
    ang = pos_ref[0].astype(f32) * invf_ref[...]
    cos_ref[0] = jnp.cos(ang)
    sin_ref[0] = jnp.sin(ang) * sign_ref[...]


def _rope_tables(positions):
    b, s = positions.shape
    ts = min(s, 512)
    inv_freq = ROPE_THETA ** (-jnp.arange(0, HEAD_DIM, 2, dtype=f32) / HEAD_DIM)
    invf = jnp.tile(inv_freq, LANE // (HEAD_DIM // 2)).reshape(1, LANE)
    sign = np.where((np.arange(LANE) % HEAD_DIM) < HEAD_DIM // 2, -1.0, 1.0).astype(np.float32).reshape(1, LANE)
    tab = jax.ShapeDtypeStruct((b, s, LANE), f32)
    return pl.pallas_call(
        _rope_table_kernel,
        out_shape=(tab, tab),
        grid=(b, s // ts),
        in_specs=[pl.BlockSpec((1, ts, 1), lambda i, j: (i, j, 0)),
                  pl.BlockSpec((1, LANE), lambda i, j: (0, 0)),
                  pl.BlockSpec((1, LANE), lambda i, j: (0, 0))],
        out_specs=(pl.BlockSpec((1, ts, LANE), lambda i, j: (i, j, 0)),
                   pl.BlockSpec((1, ts, LANE), lambda i, j: (i, j, 0))),
        name="rope_tables",
    )(positions.reshape(b, s, 1), invf, jnp.asarray(sign))


def _in_proj_kernel(x_ref, mod_ref, nw_ref, win_ref, qln_ref, wq_ref, cw_ref, cb_ref, gb_ref, cos_ref, sin_ref,
                    qt_ref, qit_ref, k_ref, vt_ref, kidx_ref, wt_ref, gt_ref, qm_ref, km_ref, vm_ref, om_ref,
                    buf_ref):
    tm = x_ref.shape[1]

    @pl.when(pl.program_id(1) == 0)
    def _():
        buf_ref[0:8, :] = jnp.zeros((8, 2 * ML_WIDTH), f32)

    x = x_ref[0]
    sh = mod_ref[0, 0:1, :]
    sc = mod_ref[0, 1:2, :]
    hb = (_rms(x, nw_ref[...]) * (1.0 + sc) + sh).astype(bf16)

    cos = cos_ref[0]
    sin = sin_ref[0]
    reps = ATT_WIDTH // LANE
    cos_w = jnp.concatenate([cos] * reps, axis=1)
    sin_w = jnp.concatenate([sin] * reps, axis=1)

    cq = _dot(hb, win_ref[:, OFF_CQ:OFF_CQ + Q_RANK])
    cqn = _rms(cq, qln_ref[...]).astype(bf16)
    qq = _dot(cqn, wq_ref[...])
    q_t = (_rope(qq[:, :ATT_WIDTH], cos_w, sin_w) * (HEAD_DIM ** -0.5 * LOG2E)).T.astype(bf16)
    qi_t = _rope(qq[:, ATT_WIDTH:], cos_w, sin_w).T.astype(bf16)
    qt_ref[...] = jnp.zeros(qt_ref.shape, bf16)
    for h in range(ATT_HEADS):
        d0 = (h % 2) * HEAD_DIM
        qt_ref[0, 0, h // 2, d0:d0 + HEAD_DIM, (h % 2) * tm:(h % 2 + 1) * tm] = q_t[h * HEAD_DIM:(h + 1) * HEAD_DIM, :]
    for h in range(IDX_HEADS):
        qit_ref[0, 0, :, h * tm:(h + 1) * tm] = qi_t[h * HEAD_DIM:(h + 1) * HEAD_DIM, :]
    k_ref[0] = _rope(_dot(hb, win_ref[:, OFF_K:OFF_K + ATT_WIDTH]), cos_w, sin_w).astype(bf16)
    vt_ref[0] = _dot(hb, win_ref[:, OFF_V:OFF_V + ATT_WIDTH]).T.astype(bf16)

    m = _dot(hb, win_ref[:, OFF_MISC:OFF_MISC + LANE])
    lane = lax.broadcasted_iota(jnp.int32, m.shape, 1)
    roped = _rope(m, cos, sin)
    biased = m + gb_ref[...]
    logsig = jnp.minimum(biased, 0.0) - jnp.log1p(jnp.exp(-jnp.abs(biased)))
    w_scale = (IDX_HEADS * HEAD_DIM) ** -0.5
    misc = jnp.where(lane < MISC_W, roped,
                     jnp.where(lane < MISC_I, m * w_scale,
                               jnp.where(lane < MISC_F, biased,
                                         jnp.where(lane < MISC_END, logsig, 0.0))))
    misc_t = misc.T
    kidx_ref[0] = misc[:, 0:HEAD_DIM].astype(bf16)
    wt_ref[0] = misc_t[MISC_W:MISC_I, :]
    gt_ref[0] = misc_t[MISC_I:MISC_END, :]

    pre = _dot(hb, win_ref[:, OFF_QKM:OFF_QKM + 2 * ML_WIDTH])
    buf_ref[8:8 + tm, :] = pre
    acc = jnp.broadcast_to(cb_ref[...], pre.shape)
    for t in range(CONV_W):
        acc = acc + cw_ref[t:t + 1, :] * buf_ref[8 - (CONV_W - 1) + t:8 - (CONV_W - 1) + t + tm, :]
    buf_ref[0:8, :] = buf_ref[tm:tm + 8, :]
    qk = acc * jax.nn.sigmoid(acc)
    qm_ref[0] = (qk[:, :ML_WIDTH] * (ML_DIM ** -0.5)).astype(bf16)
    km_ref[0] = qk[:, ML_WIDTH:].astype(bf16)
    vm_ref[0] = _dot(hb, win_ref[:, OFF_VM:OFF_VM + ML_WIDTH]).astype(bf16)
    om_ref[0] = _dot(hb, win_ref[:, OFF_OM:OFF_OM + ML_WIDTH])


def _in_proj(x, mod_l, nw, win, qln, wq, cw, cb, gb, cos, sin, tm, layer):
    b, s, d = x.shape
    nt = s // tm

    def tok(w):
        return pl.BlockSpec((1, tm, w), lambda i, j: (i, j, 0))

    def tok_t(r):
        return pl.BlockSpec((1, r, tm), lambda i, j: (i, 0, j))

    def full(a):
        return pl.BlockSpec(a.shape, lambda i, j: (0,) * a.ndim)

    def of_layer(a):
        return pl.BlockSpec((None,) + a.shape[1:], lambda i, j: (layer,) + (0,) * (a.ndim - 1))

    sd = jax.ShapeDtypeStruct
    qt_shape = (ATT_HEADS // 2, 2 * HEAD_DIM, 2 * tm)
    qit_shape = (HEAD_DIM, IDX_HEADS * tm)
    out_shape = (sd((b, nt) + qt_shape, bf16), sd((b, nt) + qit_shape, bf16), sd((b, s, ATT_WIDTH), bf16),
                 sd((b, ATT_WIDTH, s), bf16), sd((b, s, HEAD_DIM), bf16), sd((b, IDX_HEADS, s), f32),
                 sd((b, 2 * ML_HEADS, s), f32), sd((b, s, ML_WIDTH), bf16),
                 sd((b, s, ML_WIDTH), bf16), sd((b, s, ML_WIDTH), bf16), sd((b, s, ML_WIDTH), f32))
    out_specs = (pl.BlockSpec((1, 1) + qt_shape, lambda i, j: (i, j, 0, 0, 0)),
                 pl.BlockSpec((1, 1) + qit_shape, lambda i, j: (i, j, 0, 0)),
                 tok(ATT_WIDTH), tok_t(ATT_WIDTH), tok(HEAD_DIM), tok_t(IDX_HEADS),
                 tok_t(2 * ML_HEADS), tok(ML_WIDTH), tok(ML_WIDTH), tok(ML_WIDTH), tok(ML_WIDTH))
    return pl.pallas_call(
        _in_proj_kernel,
        out_shape=out_shape,
        grid=(b, nt),
        in_specs=[tok(d), pl.BlockSpec((1, 6, d), lambda i, j: (i, 0, 0)), full(nw), of_layer(win), full(qln),
                  of_layer(wq),
                  full(cw), full(cb), full(gb), tok(LANE), tok(LANE)],
        out_specs=out_specs,
        scratch_shapes=[pltpu.VMEM((tm + 8, 2 * ML_WIDTH), f32)],
        compiler_params=pltpu.CompilerParams(dimension_semantics=("arbitrary", "arbitrary"),
                                             vmem_limit_bytes=VMEM_LIMIT),
        name="in_proj",
    )(x, mod_l, nw, win, qln, wq, cw, cb, gb, cos, sin)


def _key_to_f32(key):
    bits = jnp.where(key >= 0, key, key ^ jnp.int32(0x7FFFFFFF))
    return lax.bitcast_convert_type(bits, f32)


def _fold8(x, op):
    parts = [x[i * 8:(i + 1) * 8] for i in range(x.shape[0] // 8)]
    n_acc = min(4, len(parts))
    accs = parts[:n_acc]
    for i, part in enumerate(parts[n_acc:]):
        accs[i % n_acc] = op(accs[i % n_acc], part)
    while len(accs) > 1:
        accs = [op(accs[i], accs[i + 1]) for i in range(0, len(accs), 2)]
    return accs[0]


def _dsa_kernel(qit_ref, kidx_ref, wt_ref, qt_ref, k_ref, vt_ref, nw_ref, o_ref, sc_ref, lo_ref, hi_ref, *, n_sel):
    s_len = k_ref.shape[1]
    n_rb = s_len // QB
    rg = qt_ref.shape[-1] // 2
    n_rg = s_len // rg
    sup = rg
    kf = float(n_sel)
    key_i = lax.broadcasted_iota(jnp.int32, (QB, QB), 0)
    row_i = lax.broadcasted_iota(jnp.int32, (QB, QB), 1)
    key_g = lax.broadcasted_iota(jnp.int32, (QB, rg), 0)
    row_g = lax.broadcasted_iota(jnp.int32, (QB, rg), 1)

    def col_sum(x):
        return jnp.sum(x, axis=0, keepdims=True)

    def score_rows(g, carry):
        goff = pl.multiple_of(g * rg, rg)
        w = wt_ref[0, :, pl.ds(goff, rg)]

        def score_keys(c, carry2):
            for p in range(sup // QB):
                off = pl.multiple_of(c * sup + p * QB, QB)
                lg = _dot(kidx_ref[0, pl.ds(off, QB), :], qit_ref[0, g])
                acc = jnp.zeros((QB, rg), f32)
                for h in range(IDX_HEADS):
                    acc = acc + w[h:h + 1, :] * jnp.maximum(lg[:, h * rg:(h + 1) * rg], 0.0)
                sc_ref[pl.ds(off, QB), pl.ds(goff, rg)] = jnp.where(off + key_g <= goff + row_g, acc, -jnp.inf)
            return carry2

        lax.fori_loop(0, g + 1, score_keys, 0)
        return carry

    lax.fori_loop(0, n_rg, score_rows, 0)

    first = n_sel // QB
    ones_lhs = jnp.ones((16, s_len), bf16)

    def floor_avg(a, b):
        return (a & b) + ((a ^ b) >> 1)

    if (rg // QB) % 2 == 0:
        units = [(first, 1)] if first % 2 else []
        units += [(r, 2) for r in range(first + first % 2, n_rb - 1, 2)]
        if (n_rb - first - first % 2) % 2:
            units.append((n_rb - 1, 1))
    else:
        units = [(r, 1) for r in range(first, n_rb)]

    def search_pass(_, carry):
        for r, width in units:
            blocks = range(r, r + width)
            lo = jnp.concatenate([lo_ref[b, 0:1, :] for b in blocks], axis=1)
            hi = jnp.concatenate([hi_ref[b, 0:1, :] for b in blocks], axis=1)
            mid = floor_avg(lo, hi)
            n_keys = (r + width) * QB
            hit = jnp.where(sc_ref[0:n_keys, r * QB:(r + width) * QB] >= _key_to_f32(mid), 1.0, 0.0).astype(bf16)
            ge = _dot(ones_lhs[:, :n_keys], hit)[0:1] >= kf
            new_lo = jnp.where(ge, mid, lo)
            new_hi = jnp.where(ge, hi, mid)
            for i, b in enumerate(blocks):
                lo_ref[b, 0:1, :] = new_lo[:, i * QB:(i + 1) * QB]
                hi_ref[b, 0:1, :] = new_hi[:, i * QB:(i + 1) * QB]
        return carry

    lo_ref[...] = jnp.full(lo_ref.shape, KEY_NEG_INF, jnp.int32)
    hi_ref[...] = jnp.full(hi_ref.shape, KEY_POS_INF, jnp.int32)
    lax.fori_loop(0, 32, search_pass, 0)

    def finalize(r, carry):
        roff = pl.multiple_of(r * QB, QB)
        thr = _key_to_f32(lo_ref[r, 0:1, :])

        def count(pred):
            def body(c, part):
                off = pl.multiple_of(c * QB, QB)
                blk = sc_ref[pl.ds(off, QB), pl.ds(roff, QB)]
                return part + jnp.where(pred(blk, off + key_i), 1.0, 0.0)
            return col_sum(lax.fori_loop(0, r + 1, body, jnp.zeros((QB, QB), f32)))

        finite = thr > -jnp.inf
        tied = jnp.logical_and(count(lambda blk, pos: blk >= thr) != kf, finite)
        cut0 = jnp.where(finite, s_len, -1)

        def break_ties(_):
            need = kf - count(lambda blk, pos: blk > thr)

            def bisect_idx(_, lh):
                lo_j, hi_j = lh
                mid = (lo_j + hi_j) >> 1
                ok = count(lambda blk, pos: jnp.logical_and(blk == thr, pos <= mid)) >= need
                return jnp.where(ok, lo_j, mid), jnp.where(ok, mid, hi_j)

            n_it = int(np.ceil(np.log2(s_len + 1)))
            _, hi_j = lax.fori_loop(0, n_it, bisect_idx, (jnp.full((1, QB), -1, jnp.int32),
                                                          jnp.full((1, QB), s_len - 1, jnp.int32)))
            return jnp.where(tied, hi_j, cut0)

        cut = lax.cond(jnp.max(tied.astype(jnp.int32)) > 0, break_ties, lambda _: cut0, 0)

        def write_mask(c, carry2):
            off = pl.multiple_of(c * QB, QB)
            blk = sc_ref[pl.ds(off, QB), pl.ds(roff, QB)]
            sel = jnp.logical_or(blk > thr, jnp.logical_and(blk == thr, off + key_i <= cut))
            sc_ref[pl.ds(off, QB), pl.ds(roff, QB)] = jnp.where(sel, 0.0, NEG_BIG)
            return carry2

        lax.fori_loop(0, per_g * (r // per_g + 1), write_mask, 0)
        return carry

    per_g = rg // QB

    any_tied = jnp.zeros((1, QB), jnp.int32)
    for r, width in units:
        n_keys = (r + width) * QB
        thr = jnp.concatenate([_key_to_f32(lo_ref[b, 0:1, :]) for b in range(r, r + width)], axis=1)
        hit = jnp.where(sc_ref[0:n_keys, r * QB:(r + width) * QB] >= thr, 1.0, 0.0).astype(bf16)
        over = (_dot(ones_lhs[:, :n_keys], hit)[0:1] != kf).astype(jnp.int32)
        for i in range(width):
            any_tied = jnp.maximum(any_tied, over[:, i * QB:(i + 1) * QB])

    def with_ties(_):
        lax.fori_loop(0, n_rb, finalize, 0)
        return 0

    def without_ties(_):
        for r in range(n_rb):
            thr = _key_to_f32(lo_ref[r, 0:1, :])
            thr = jnp.maximum(thr, float(jnp.finfo(f32).min))
            for c in range(per_g * (r // per_g + 1)):
                tile = (slice(c * QB, (c + 1) * QB), slice(r * QB, (r + 1) * QB))
                sc_ref[tile] = jnp.where(sc_ref[tile] >= thr, 0.0, NEG_BIG)
        return 0

    lax.cond(jnp.max(any_tied) > 0, with_ties, without_ties, 0)

    def attend(g, carry, sup=sup // ATT_KEY_SPLIT):
        goff = pl.multiple_of(g * rg, rg)
        n_sup = (g + 1) * ATT_KEY_SPLIT

        ones_rows = jnp.ones((16, sup), bf16)
        outs = []
        for h0 in range(0, ATT_HEADS, ATT_HEADS_PER_LOOP):
            heads = range(h0, h0 + ATT_HEADS_PER_LOOP)

            def key_step(c, carry2, heads=heads):
                m_run, acc = carry2
                off = pl.multiple_of(c * sup, sup)
                bias = sc_ref[pl.ds(off, sup), pl.ds(goff, rg)]
                s_all = []
                for hp in range(heads[0] // 2, heads[-1] // 2 + 1):
                    k_pair = k_ref[0, pl.ds(off, sup), hp * 2 * HEAD_DIM:(hp + 1) * 2 * HEAD_DIM]
                    s2 = _dot(k_pair, qt_ref[0, g, hp])
                    s_all += [s2[:, :rg] + bias, s2[:, rg:] + bias]
                m_new, acc_new = [], []
                for i, h in enumerate(heads):
                    m_h = jnp.maximum(m_run[i], jnp.max(_fold8(s_all[i], jnp.maximum), axis=0, keepdims=True))
                    alpha = jnp.exp2(m_run[i] - m_h)
                    p = jnp.exp2(s_all[i] - m_h).astype(bf16)
                    vt_h = jnp.concatenate([vt_ref[0, h * HEAD_DIM:(h + 1) * HEAD_DIM, pl.ds(off, sup)], ones_rows],
                                           axis=0)
                    acc_new.append(alpha * acc[i] + _dot(vt_h, p))
                    m_new.append(m_h)
                return tuple(m_new), tuple(acc_new)

            _, acc = lax.fori_loop(0, n_sup, key_step,
                                   (tuple(jnp.full((1, rg), NEG_BIG, f32) for _ in heads),
                                    tuple(jnp.zeros((HEAD_DIM + 16, rg), f32) for _ in heads)))
            for i, h in enumerate(heads):
                o = acc[i][:HEAD_DIM] / acc[i][HEAD_DIM:HEAD_DIM + 1]
                inv = lax.rsqrt(jnp.mean(o * o, axis=0, keepdims=True) + EPS)
                outs.append(o * inv * nw_ref[h * HEAD_DIM:(h + 1) * HEAD_DIM, :])
        o_ref[0, pl.ds(goff, rg), :] = jnp.concatenate(outs, axis=0).T.astype(bf16)
        return carry

    lax.fori_loop(0, n_rg, attend, 0)


def _dsa(qt, qit, k, vt, kidx, wt, nw):
    b, s, _ = k.shape
    n_rb = s // QB
    n_sel = min(TOPK, s // 4)
    rg = qt.shape[-1] // 2
    s_pad = -(-s // rg) * rg
    kernel = functools.partial(_dsa_kernel, n_sel=n_sel)

    def per_batch(a):
        return pl.BlockSpec((1,) + a.shape[1:], lambda i: (i,) + (0,) * (a.ndim - 1))

    nw_col = nw.reshape(ATT_WIDTH, 1)
    return pl.pallas_call(
        kernel,
        out_shape=jax.ShapeDtypeStruct((b, s, ATT_WIDTH), bf16),
        grid=(b,),
        in_specs=[per_batch(qit), per_batch(kidx), per_batch(wt), per_batch(qt), per_batch(k), per_batch(vt),
                  pl.BlockSpec((ATT_WIDTH, 1), lambda i: (0, 0))],
        out_specs=pl.BlockSpec((1, s, ATT_WIDTH), lambda i: (i, 0, 0)),
        scratch_shapes=[pltpu.VMEM((s_pad, s), f32),
                        pltpu.VMEM((n_rb, 8, QB), jnp.int32),
                        pltpu.VMEM((n_rb, 8, QB), jnp.int32)],
        compiler_params=pltpu.CompilerParams(dimension_semantics=("arbitrary",),
                                             vmem_limit_bytes=VMEM_LIMIT),
        name="dsa_attention",
    )(qit, kidx, wt, qt, k, vt, nw_col)


ML_STATE_ROWS = ML_DIM + 16


def _mlstm_kernel(qm_ref, km_ref, vm_ref, om_ref, gt_ref, nw_ref, hm_ref, c_ref, *, chunk):
    s_len = qm_ref.shape[1]
    n_chunks = s_len // chunk
    s_i = lax.broadcasted_iota(jnp.int32, (chunk, chunk), 0)
    t_i = lax.broadcasted_iota(jnp.int32, (chunk, chunk), 1)
    causal = s_i <= t_i
    tri_t = jnp.where(causal, 1.0, 0.0).astype(f32)
    first_row = lax.broadcasted_iota(jnp.int32, (ML_STATE_ROWS - ML_DIM, chunk), 0) == 0
    c_ref[...] = jnp.zeros(c_ref.shape, f32)

    def body(c, m_state):
        off = pl.multiple_of(c * chunk, chunk)
        g_row = gt_ref[0, :, pl.ds(off, chunk)]
        cum_row = _dot(g_row, tri_t, precision=lax.Precision.HIGHEST)
        g_src = g_row[:ML_HEADS] - cum_row[ML_HEADS:]
        g_src_col = jnp.concatenate([g_src, jnp.zeros((LANE - ML_HEADS, chunk), f32)], axis=0).T
        m_next = []
        for h in range(ML_HEADS):
            ig_row = g_row[h:h + 1, :]
            b_row = cum_row[ML_HEADS + h:ML_HEADS + h + 1, :]
            m_s = m_state[h]
            total = b_row[:, chunk - 1:chunk]
            hs = slice(h * ML_DIM, (h + 1) * ML_DIM)
            q_t = qm_ref[0, pl.ds(off, chunk), hs].astype(f32).T.astype(bf16)
            k_h = km_ref[0, pl.ds(off, chunk), hs]
            v_tf = vm_ref[0, pl.ds(off, chunk), hs].astype(f32).T

            d_t = jnp.where(causal, g_src_col[:, h:h + 1] + b_row, -jnp.inf)
            inter = b_row + m_s
            m_row = jnp.maximum(inter, jnp.max(d_t, axis=0, keepdims=True))
            w_inter = jnp.exp(inter - m_row)
            s_t = _dot(k_h, q_t) * jnp.exp(d_t - m_row)
            c_ext = c_ref[h]
            qc = _dot(c_ext.astype(bf16), q_t)
            num = _dot(v_tf.astype(bf16), s_t.astype(bf16)) + w_inter * qc[:ML_DIM]
            den = jnp.sum(s_t, axis=0, keepdims=True) + w_inter * qc[ML_DIM:ML_DIM + 1]
            hh = num / jnp.maximum(jnp.abs(den), jnp.exp(-m_row))

            w_log = total - b_row + ig_row
            m_new = jnp.maximum(total + m_s, jnp.max(w_log, axis=1, keepdims=True))
            decay = jnp.exp(total + m_s - m_new)
            w_state = jnp.exp(w_log - m_new)
            lhs = jnp.concatenate([(v_tf * w_state).astype(bf16),
                                   jnp.where(first_row, w_state, 0.0).astype(bf16)], axis=0)
            c_ref[h] = decay * c_ext + _dot(lhs, k_h)
            m_next.append(m_new)

            inv = lax.rsqrt(jnp.mean(hh * hh, axis=0, keepdims=True) + EPS)
            gate = jax.nn.sigmoid(om_ref[0, pl.ds(off, chunk), hs].T)
            out_t = gate * hh * inv * nw_ref[hs, :]
            hm_ref[0, pl.ds(off, chunk), hs] = out_t.T.astype(bf16)
        return tuple(m_next)

    unroll = next(u for u in (4, 2, 1) if n_chunks % u == 0)
    lax.fori_loop(0, n_chunks, body, tuple(jnp.zeros((1, 1), f32) for _ in range(ML_HEADS)), unroll=unroll)


def _mlstm(qm, km, vm, om, gt, nw):
    b, s, _ = qm.shape
    chunk = min(ML_CHUNK, s)
    kernel = functools.partial(_mlstm_kernel, chunk=chunk)
    tok = pl.BlockSpec((1, s, ML_WIDTH), lambda i: (i, 0, 0))
    return pl.pallas_call(
        kernel,
        out_shape=jax.ShapeDtypeStruct((b, s, ML_WIDTH), bf16),
        grid=(b,),
        in_specs=[tok, tok, tok, tok,
                  pl.BlockSpec((1, 2 * ML_HEADS, s), lambda i: (i, 0, 0)),
                  pl.BlockSpec((ML_WIDTH, 1), lambda i: (0, 0))],
        out_specs=tok,
        scratch_shapes=[pltpu.VMEM((ML_HEADS, ML_STATE_ROWS, ML_DIM), f32)],
        compiler_params=pltpu.CompilerParams(dimension_semantics=("arbitrary",),
                                             vmem_limit_bytes=VMEM_LIMIT),
        name="mlstm",
    )(qm, km, vm, om, gt, nw.reshape(ML_WIDTH, 1))


def _out_ffn_kernel(att_ref, hm_ref, x_ref, mod_ref, wout_ref, npost_ref, fpre_ref, wgu_ref, wdn_ref, fpost_ref,
                    o_ref, *, bounds):
    hid = wdn_ref.shape[0]
    g_m = mod_ref[0, 2:3, :]
    sh_f = mod_ref[0, 3:4, :]
    sc_f = mod_ref[0, 4:5, :]
    g_f = mod_ref[0, 5:6, :]
    tm = x_ref.shape[1]
    piece = tm // FFN_ROW_PIECES
    for r0 in range(0, tm, piece):
        rows = slice(r0, r0 + piece)
        y = _dot(att_ref[0, rows, :], wout_ref[0:ATT_WIDTH, :]) + _dot(hm_ref[0, rows, :], wout_ref[ATT_WIDTH:, :])
        x1 = x_ref[0, rows, :] + g_m * _rms(y, npost_ref[...])
        h2 = (_rms(x1, fpre_ref[...]) * (1.0 + sc_f) + sh_f).astype(bf16)
        y2 = jnp.zeros(x1.shape, f32)
        for c0, c1 in zip(bounds[:-1], bounds[1:]):
            g = _dot(h2, wgu_ref[:, c0:c1])
            u = _dot(h2, wgu_ref[:, hid + c0:hid + c1])
            a = (g * jax.nn.sigmoid(g) * u).astype(bf16)
            y2 = y2 + _dot(a, wdn_ref[c0:c1, :])
        o_ref[0, rows, :] = x1 + g_f * _rms(y2, fpost_ref[...])


def _out_ffn(att, hm, x, mod_l, wout, npost, fpre, wgu, wdn, fpost, tm, layer):
    b, s, d = x.shape
    hid = wdn.shape[1]
    mxu_tile = 256
    cut = -(-(hid // 2) // mxu_tile) * mxu_tile
    bounds = (0, cut, hid) if 0 < cut < hid else (0, hid)

    def tok(w):
        return pl.BlockSpec((1, tm, w), lambda i, j: (i, j, 0))

    def const(a):
        return pl.BlockSpec(a.shape, lambda i, j: (0,) * a.ndim, pipeline_mode=pl.Buffered(1))

    def of_layer(a):
        return pl.BlockSpec((None,) + a.shape[1:], lambda i, j: (layer,) + (0,) * (a.ndim - 1),
                            pipeline_mode=pl.Buffered(1))

    kernel = functools.partial(_out_ffn_kernel, bounds=bounds)
    return pl.pallas_call(
        kernel,
        out_shape=jax.ShapeDtypeStruct((b, s, d), f32),
        grid=(b, s // tm),
        in_specs=[tok(ATT_WIDTH), tok(ML_WIDTH), tok(d), pl.BlockSpec((1, 6, d), lambda i, j: (i, 0, 0)),
                  of_layer(wout), const(npost), const(fpre), of_layer(wgu), of_layer(wdn), const(fpost)],
        out_specs=tok(d),
        compiler_params=pltpu.CompilerParams(dimension_semantics=("arbitrary", "arbitrary"),
                                             vmem_limit_bytes=VMEM_LIMIT),
        name="out_ffn",
    )(att, hm, x, mod_l, wout, npost, fpre, wgu, wdn, fpost)


def _arrange_w_in(w):
    pts = [int(p) for p in np.cumsum(IN_SPLITS)[:-1]]
    cq, k_att, v_att, k_idx, w_idx, q_m, k_m, v_m, o_m, i_m, f_m = jnp.split(w, pts, axis=-1)
    misc = jnp.concatenate([k_idx, w_idx, i_m, f_m, jnp.zeros(w.shape[:-1] + (LANE - MISC_END,), w.dtype)], axis=-1)
    return jnp.concatenate([cq, k_att, v_att, misc, q_m, k_m, v_m, o_m], axis=-1).astype(bf16)


def kernel(x, c, positions, w_mod, b_mod, mix_norm_pre, mix_norm_post, w_in, q_latent_norm, w_q_up, w_qidx_up,
           conv_w, conv_b, b_igate, b_fgate, attn_out_norm, mlstm_out_norm, w_out, ffn_norm_pre, ffn_norm_post,
           w_gate_up, w_down):
    depth = w_mod.shape[0]
    b, s, d = x.shape
    tm = min(512, s)
    mod = _modulation(c, w_mod, b_mod).reshape(depth, b, 6, d)
    cos, sin = _rope_tables(positions)

    def row(a):
        return a.reshape(1, -1)

    win = _arrange_w_in(w_in)
    wq = jnp.concatenate([w_q_up, w_qidx_up], axis=-1).astype(bf16)
    wout, wgu, wdn = w_out.astype(bf16), w_gate_up.astype(bf16), w_down.astype(bf16)
    for l in range(depth):
        gb = jnp.zeros((1, LANE), f32).at[0, MISC_I:MISC_F].set(b_igate[l]).at[0, MISC_F:MISC_END].set(b_fgate[l])
        qt, qit, k, vt, kidx, wt, gt, qm, km, vm, om = _in_proj(
            x, mod[l], row(mix_norm_pre[l]), win, row(q_latent_norm[l]), wq, conv_w[l], row(conv_b[l]), gb,
            cos, sin, tm, l)
        att = _dsa(qt, qit, k, vt, kidx, wt, attn_out_norm[l])
        hm = _mlstm(qm, km, vm, om, gt, mlstm_out_norm[l])
        x = _out_ffn(att, hm, x, mod[l], wout, row(mix_norm_post[l]), row(ffn_norm_pre[l]), wgu, wdn,
                     row(ffn_norm_post[l]), tm, l)
    return x
```

```python
import functools

import numpy as np
import jax
import jax.numpy as jnp
from jax import lax
from jax.experimental import pallas as pl
from jax.experimental.pallas import tpu as pltpu

EPS = 1e-6
ROPE_THETA = 10000.0
ATT_HEADS = 8
HEAD_DIM = 64
ATT_WIDTH = ATT_HEADS * HEAD_DIM
Q_RANK = 256
IDX_HEADS = 8
ML_HEADS = 4
ML_DIM = 128
ML_WIDTH = ML_HEADS * ML_DIM
CONV_W = 4
TOPK = 256
QB = 128
ATT_HEADS_PER_LOOP = 8
ATT_KEY_SPLIT = 1
FFN_ROW_PIECES = 1
LANE = 128
ML_CHUNK = 256
IN_SPLITS = (Q_RANK, ATT_WIDTH, ATT_WIDTH, HEAD_DIM, IDX_HEADS,
             ML_WIDTH, ML_WIDTH, ML_WIDTH, ML_WIDTH, ML_HEADS, ML_HEADS)

OFF_CQ = 0
OFF_K = OFF_CQ + Q_RANK
OFF_V = OFF_K + ATT_WIDTH
OFF_MISC = OFF_V + ATT_WIDTH
OFF_QKM = OFF_MISC + LANE
OFF_VM = OFF_QKM + 2 * ML_WIDTH
OFF_OM = OFF_VM + ML_WIDTH
IN_COLS_PADDED = OFF_OM + ML_WIDTH
MISC_W = HEAD_DIM
MISC_I = MISC_W + IDX_HEADS
MISC_F = MISC_I + ML_HEADS
MISC_END = MISC_F + ML_HEADS

VMEM_LIMIT = 56 * 1024 * 1024
NEG_BIG = -1e30
LOG2E = float(np.log2(np.e))
KEY_POS_INF = 0x7F800000
KEY_NEG_INF = (0xFF800000 - (1 << 32)) ^ 0x7FFFFFFF

bf16 = jnp.bfloat16
f32 = jnp.float32


def _rms(x, w):
    return x * lax.rsqrt(jnp.mean(x * x, axis=-1, keepdims=True) + EPS) * w


def _dot(a, b, precision=None):
    return jnp.dot(a, b, preferred_element_type=f32, precision=precision)


def _rope(x, cos, sin_signed):
    n = x.shape[1]
    lane = lax.broadcasted_iota(jnp.int32, x.shape, 1)
    low = (lane & (HEAD_DIM // 2)) == 0
    rot = jnp.where(low, pltpu.roll(x, n - HEAD_DIM // 2, 1), pltpu.roll(x, HEAD_DIM // 2, 1))
    return x * cos + rot * sin_signed


def _mod_kernel(c_ref, w_ref, b_ref, o_ref):
    c = c_ref[...]
    c_act = c * jax.nn.sigmoid(c)
    o_ref[0] = _dot(c_act, w_ref[0], precision=lax.Precision.HIGHEST) + b_ref[0]


def _modulation(c, w_mod, b_mod):
    depth, d, d6 = w_mod.shape
    b = c.shape[0]
    nj = d6 // d
    return pl.pallas_call(
        _mod_kernel,
        out_shape=jax.ShapeDtypeStruct((depth, b, d6), f32),
        grid=(depth, nj),
        in_specs=[pl.BlockSpec((b, d), lambda l, j: (0, 0)),
                  pl.BlockSpec((1, d, d), lambda l, j: (l, 0, j)),
                  pl.BlockSpec((1, 1, d), lambda l, j: (l, 0, j))],
        out_specs=pl.BlockSpec((1, b, d), lambda l, j: (l, 0, j)),
        name="modulation",
    )(c, w_mod, b_mod.reshape(depth, 1, d6))


def _rope_table_kernel(pos_ref, invf_ref, sign_ref, cos_ref, sin_ref):
    ang = pos_ref[0].astype(f32) * invf_ref[...]
    cos_ref[0] = jnp.cos(ang)
    sin_ref[0] = jnp.sin(ang) * sign_ref[...]


def _rope_tables(positions):
    b, s = positions.shape
    ts = min(s, 512)
    inv_freq = ROPE_THETA ** (-jnp.arange(0, HEAD_DIM, 2, dtype=f32) / HEAD_DIM)
    invf = jnp.tile(inv_freq, LANE // (HEAD_DIM // 2)).reshape(1, LANE)
    sign = np.where((np.arange(LANE) % HEAD_DIM) < HEAD_DIM // 2, -1.0, 1.0).astype(np.float32).reshape(1, LANE)
    tab = jax.ShapeDtypeStruct((b, s, LANE), f32)
    return pl.pallas_call(
        _rope_table_kernel,
        out_shape=(tab, tab),
        grid=(b, s // ts),
        in_specs=[pl.BlockSpec((1, ts, 1), lambda i, j: (i, j, 0)),
                  pl.BlockSpec((1, LANE), lambda i, j: (0, 0)),
                  pl.BlockSpec((1, LANE), lambda i, j: (0, 0))],
        out_specs=(pl.BlockSpec((1, ts, LANE), lambda i, j: (i, j, 0)),
                   pl.BlockSpec((1, ts, LANE), lambda i, j: (i, j, 0))),
        name="rope_tables",
    )(positions.reshape(b, s, 1), invf, jnp.asarray(sign))


def _in_proj_kernel(x_ref, mod_ref, nw_ref, win_ref, qln_ref, wq_ref, cw_ref, cb_ref, gb_ref, cos_ref, sin_ref,
                    qt_ref, qit_ref, k_ref, vt_ref, kidx_ref, wt_ref, gt_ref, qm_ref, km_ref, vm_ref, om_ref,
                    buf_ref):
    tm = x_ref.shape[1]

    @pl.when(pl.program_id(1) == 0)
    def _():
        buf_ref[0:8, :] = jnp.zeros((8, 2 * ML_WIDTH), f32)

    x = x_ref[0]
    sh = mod_ref[0, 0:1, :]
    sc = mod_ref[0, 1:2, :]
    hb = (_rms(x, nw_ref[...]) * (1.0 + sc) + sh).astype(bf16)

    cos = cos_ref[0]
    sin = sin_ref[0]
    reps = ATT_WIDTH // LANE
    cos_w = jnp.concatenate([cos] * reps, axis=1)
    sin_w = jnp.concatenate([sin] * reps, axis=1)

    cq = _dot(hb, win_ref[:, OFF_CQ:OFF_CQ + Q_RANK])
    cqn = _rms(cq, qln_ref[...]).astype(bf16)
    qq = _dot(cqn, wq_ref[...])
    q_t = (_rope(qq[:, :ATT_WIDTH], cos_w, sin_w) * (HEAD_DIM ** -0.5 * LOG2E)).T.astype(bf16)
    qi_t = _rope(qq[:, ATT_WIDTH:], cos_w, sin_w).T.astype(bf16)
    qt_ref[...] = jnp.zeros(qt_ref.shape, bf16)
    for h in range(ATT_HEADS):
        d0 = (h % 2) * HEAD_DIM
        qt_ref[0, 0, h // 2, d0:d0 + HEAD_DIM, (h % 2) * tm:(h % 2 + 1) * tm] = q_t[h * HEAD_DIM:(h + 1) * HEAD_DIM, :]
    for h in range(IDX_HEADS):
        qit_ref[0, 0, :, h * tm:(h + 1) * tm] = qi_t[h * HEAD_DIM:(h + 1) * HEAD_DIM, :]
    k_ref[0] = _rope(_dot(hb, win_ref[:, OFF_K:OFF_K + ATT_WIDTH]), cos_w, sin_w).astype(bf16)
    vt_ref[0] = _dot(hb, win_ref[:, OFF_V:OFF_V + ATT_WIDTH]).T.astype(bf16)

    m = _dot(hb, win_ref[:, OFF_MISC:OFF_MISC + LANE])
    lane = lax.broadcasted_iota(jnp.int32, m.shape, 1)
    roped = _rope(m, cos, sin)
    biased = m + gb_ref[...]
    logsig = jnp.minimum(biased, 0.0) - jnp.log1p(jnp.exp(-jnp.abs(biased)))
    w_scale = (IDX_HEADS * HEAD_DIM) ** -0.5
    misc = jnp.where(lane < MISC_W, roped,
                     jnp.where(lane < MISC_I, m * w_scale,
                               jnp.where(lane < MISC_F, biased,
                                         jnp.where(lane < MISC_END, logsig, 0.0))))
    misc_t = misc.T
    kidx_ref[0] = misc[:, 0:HEAD_DIM].astype(bf16)
    wt_ref[0] = misc_t[MISC_W:MISC_I, :]
    gt_ref[0] = misc_t[MISC_I:MISC_END, :]

    pre = _dot(hb, win_ref[:, OFF_QKM:OFF_QKM + 2 * ML_WIDTH])
    buf_ref[8:8 + tm, :] = pre
    acc = jnp.broadcast_to(cb_ref[...], pre.shape)
    for t in range(CONV_W):
        acc = acc + cw_ref[t:t + 1, :] * buf_ref[8 - (CONV_W - 1) + t:8 - (CONV_W - 1) + t + tm, :]
    buf_ref[0:8, :] = buf_ref[tm:tm + 8, :]
    qk = acc * jax.nn.sigmoid(acc)
    qm_ref[0] = (qk[:, :ML_WIDTH] * (ML_DIM ** -0.5)).astype(bf16)
    km_ref[0] = qk[:, ML_WIDTH:].astype(bf16)
    vm_ref[0] = _dot(hb, win_ref[:, OFF_VM:OFF_VM + ML_WIDTH]).astype(bf16)
    om_ref[0] = _dot(hb, win_ref[:, OFF_OM:OFF_OM + ML_WIDTH])


def _in_proj(x, mod_l, nw, win, qln, wq, cw, cb, gb, cos, sin, tm, layer):
    b, s, d = x.shape
    nt = s // tm

    def tok(w):
        return pl.BlockSpec((1, tm, w), lambda i, j: (i, j, 0))

    def tok_t(r):
        return pl.BlockSpec((1, r, tm), lambda i, j: (i, 0, j))

    def full(a):
        return pl.BlockSpec(a.shape, lambda i, j: (0,) * a.ndim)

    def of_layer(a):
        return pl.BlockSpec((None,) + a.shape[1:], lambda i, j: (layer,) + (0,) * (a.ndim - 1))

    sd = jax.ShapeDtypeStruct
    qt_shape = (ATT_HEADS // 2, 2 * HEAD_DIM, 2 * tm)
    qit_shape = (HEAD_DIM, IDX_HEADS * tm)
    out_shape = (sd((b, nt) + qt_shape, bf16), sd((b, nt) + qit_shape, bf16), sd((b, s, ATT_WIDTH), bf16),
                 sd((b, ATT_WIDTH, s), bf16), sd((b, s, HEAD_DIM), bf16), sd((b, IDX_HEADS, s), f32),
                 sd((b, 2 * ML_HEADS, s), f32), sd((b, s, ML_WIDTH), bf16),
                 sd((b, s, ML_WIDTH), bf16), sd((b, s, ML_WIDTH), bf16), sd((b, s, ML_WIDTH), f32))
    out_specs = (pl.BlockSpec((1, 1) + qt_shape, lambda i, j: (i, j, 0, 0, 0)),
                 pl.BlockSpec((1, 1) + qit_shape, lambda i, j: (i, j, 0, 0)),
                 tok(ATT_WIDTH), tok_t(ATT_WIDTH), tok(HEAD_DIM), tok_t(IDX_HEADS),
                 tok_t(2 * ML_HEADS), tok(ML_WIDTH), tok(ML_WIDTH), tok(ML_WIDTH), tok(ML_WIDTH))
    return pl.pallas_call(
        _in_proj_kernel,
        out_shape=out_shape,
        grid=(b, nt),
        in_specs=[tok(d), pl.BlockSpec((1, 6, d), lambda i, j: (i, 0, 0)), full(nw), of_layer(win), full(qln),
                  of_layer(wq),
                  full(cw), full(cb), full(gb), tok(LANE), tok(LANE)],
        out_specs=out_specs,
        scratch_shapes=[pltpu.VMEM((tm + 8, 2 * ML_WIDTH), f32)],
        compiler_params=pltpu.CompilerParams(dimension_semantics=("arbitrary", "arbitrary"),
                                             vmem_limit_bytes=VMEM_LIMIT),
        name="in_proj",
    )(x, mod_l, nw, win, qln, wq, cw, cb, gb, cos, sin)


def _key_to_f32(key):
    bits = jnp.where(key >= 0, key, key ^ jnp.int32(0x7FFFFFFF))
    return lax.bitcast_convert_type(bits, f32)


def _fold8(x, op):
    parts = [x[i * 8:(i + 1) * 8] for i in range(x.shape[0] // 8)]
    n_acc = min(4, len(parts))
    accs = parts[:n_acc]
    for i, part in enumerate(parts[n_acc:]):
        accs[i % n_acc] = op(accs[i % n_acc], part)
    while len(accs) > 1:
        accs = [op(accs[i], accs[i + 1]) for i in range(0, len(accs), 2)]
    return accs[0]


def _dsa_kernel(qit_ref, kidx_ref, wt_ref, qt_ref, k_ref, vt_ref, nw_ref, o_ref, sc_ref, lo_ref, hi_ref, *, n_sel):
    s_len = k_ref.shape[1]
    n_rb = s_len // QB
    rg = qt_ref.shape[-1] // 2
    n_rg = s_len // rg
    sup = rg
    kf = float(n_sel)
    key_i = lax.broadcasted_iota(jnp.int32, (QB, QB), 0)
    row_i = lax.broadcasted_iota(jnp.int32, (QB, QB), 1)
    key_g = lax.broadcasted_iota(jnp.int32, (QB, rg), 0)
    row_g = lax.broadcasted_iota(jnp.int32, (QB, rg), 1)

    def col_sum(x):
        return jnp.sum(x, axis=0, keepdims=True)

    def score_rows(g, carry):
        goff = pl.multiple_of(g * rg, rg)
        w = wt_ref[0, :, pl.ds(goff, rg)]

        def score_keys(c, carry2):
            for p in range(sup // QB):
                off = pl.multiple_of(c * sup + p * QB, QB)
                lg = _dot(kidx_ref[0, pl.ds(off, QB), :], qit_ref[0, g])
                acc = jnp.zeros((QB, rg), f32)
                for h in range(IDX_HEADS):
                    acc = acc + w[h:h + 1, :] * jnp.maximum(lg[:, h * rg:(h + 1) * rg], 0.0)
                sc_ref[pl.ds(off, QB), pl.ds(goff, rg)] = jnp.where(off + key_g <= goff + row_g, acc, -jnp.inf)
            return carry2

        lax.fori_loop(0, g + 1, score_keys, 0)
        return carry

    lax.fori_loop(0, n_rg, score_rows, 0)

    first = n_sel // QB
    ones_lhs = jnp.ones((16, s_len), bf16)

    def floor_avg(a, b):
        return (a & b) + ((a ^ b) >> 1)

    if (rg // QB) % 2 == 0:
        units = [(first, 1)] if first % 2 else []
        units += [(r, 2) for r in range(first + first % 2, n_rb - 1, 2)]
        if (n_rb - first - first % 2) % 2:
            units.append((n_rb - 1, 1))
    else:
        units = [(r, 1) for r in range(first, n_rb)]

    def search_pass(_, carry):
        for r, width in units:
            blocks = range(r, r + width)
            lo = jnp.concatenate([lo_ref[b, 0:1, :] for b in blocks], axis=1)
            hi = jnp.concatenate([hi_ref[b, 0:1, :] for b in blocks], axis=1)
            mid = floor_avg(lo, hi)
            n_keys = (r + width) * QB
            hit = jnp.where(sc_ref[0:n_keys, r * QB:(r + width) * QB] >= _key_to_f32(mid), 1.0, 0.0).astype(bf16)
            ge = _dot(ones_lhs[:, :n_keys], hit)[0:1] >= kf
            new_lo = jnp.where(ge, mid, lo)
            new_hi = jnp.where(ge, hi, mid)
            for i, b in enumerate(blocks):
                lo_ref[b, 0:1, :] = new_lo[:, i * QB:(i + 1) * QB]
                hi_ref[b, 0:1, :] = new_hi[:, i * QB:(i + 1) * QB]
        return carry

    lo_ref[...] = jnp.full(lo_ref.shape, KEY_NEG_INF, jnp.int32)
    hi_ref[...] = jnp.full(hi_ref.shape, KEY_POS_INF, jnp.int32)
    lax.fori_loop(0, 32, search_pass, 0)

    def finalize(r, carry):
        roff = pl.multiple_of(r * QB, QB)
        thr = _key_to_f32(lo_ref[r, 0:1, :])

        def count(pred):
            def body(c, part):
                off = pl.multiple_of(c * QB, QB)
                blk = sc_ref[pl.ds(off, QB), pl.ds(roff, QB)]
                return part + jnp.where(pred(blk, off + key_i), 1.0, 0.0)
            return col_sum(lax.fori_loop(0, r + 1, body, jnp.zeros((QB, QB), f32)))

        finite = thr > -jnp.inf
        tied = jnp.logical_and(count(lambda blk, pos: blk >= thr) != kf, finite)
        cut0 = jnp.where(finite, s_len, -1)

        def break_ties(_):
            need = kf - count(lambda blk, pos: blk > thr)

            def bisect_idx(_, lh):
                lo_j, hi_j = lh
                mid = (lo_j + hi_j) >> 1
                ok = count(lambda blk, pos: jnp.logical_and(blk == thr, pos <= mid)) >= need
                return jnp.where(ok, lo_j, mid), jnp.where(ok, mid, hi_j)

            n_it = int(np.ceil(np.log2(s_len + 1)))
            _, hi_j = lax.fori_loop(0, n_it, bisect_idx, (jnp.full((1, QB), -1, jnp.int32),
                                                          jnp.full((1, QB), s_len - 1, jnp.int32)))
            return jnp.where(tied, hi_j, cut0)

        cut = lax.cond(jnp.max(tied.astype(jnp.int32)) > 0, break_ties, lambda _: cut0, 0)

        def write_mask(c, carry2):
            off = pl.multiple_of(c * QB, QB)
            blk = sc_ref[pl.ds(off, QB), pl.ds(roff, QB)]
            sel = jnp.logical_or(blk > thr, jnp.logical_and(blk == thr, off + key_i <= cut))
            sc_ref[pl.ds(off, QB), pl.ds(roff, QB)] = jnp.where(sel, 0.0, NEG_BIG)
            return carry2

        lax.fori_loop(0, per_g * (r // per_g + 1), write_mask, 0)
        return carry

    per_g = rg // QB

    any_tied = jnp.zeros((1, QB), jnp.int32)
    for r, width in units:
        n_keys = (r + width) * QB
        thr = jnp.concatenate([_key_to_f32(lo_ref[b, 0:1, :]) for b in range(r, r + width)], axis=1)
        hit = jnp.where(sc_ref[0:n_keys, r * QB:(r + width) * QB] >= thr, 1.0, 0.0).astype(bf16)
        over = (_dot(ones_lhs[:, :n_keys], hit)[0:1] != kf).astype(jnp.int32)
        for i in range(width):
            any_tied = jnp.maximum(any_tied, over[:, i * QB:(i + 1) * QB])

    def with_ties(_):
        lax.fori_loop(0, n_rb, finalize, 0)
        return 0

    def without_ties(_):
        for r in range(n_rb):
            thr = _key_to_f32(lo_ref[r, 0:1, :])
            thr = jnp.maximum(thr, float(jnp.finfo(f32).min))
            for c in range(per_g * (r // per_g + 1)):
                tile = (slice(c * QB, (c + 1) * QB), slice(r * QB, (r + 1) * QB))
                sc_ref[tile] = jnp.where(sc_ref[tile] >= thr, 0.0, NEG_BIG)
        return 0

    lax.cond(jnp.max(any_tied) > 0, with_ties, without_ties, 0)

    def attend(g, carry, sup=sup // ATT_KEY_SPLIT):
        goff = pl.multiple_of(g * rg, rg)
        n_sup = (g + 1) * ATT_KEY_SPLIT

        ones_rows = jnp.ones((16, sup), bf16)
        outs = []
        for h0 in range(0, ATT_HEADS, ATT_HEADS_PER_LOOP):
            heads = range(h0, h0 + ATT_HEADS_PER_LOOP)

            def key_step(c, carry2, heads=heads):
                m_run, acc = carry2
                off = pl.multiple_of(c * sup, sup)
                bias = sc_ref[pl.ds(off, sup), pl.ds(goff, rg)]
                s_all = []
                for hp in range(heads[0] // 2, heads[-1] // 2 + 1):
                    k_pair = k_ref[0, pl.ds(off, sup), hp * 2 * HEAD_DIM:(hp + 1) * 2 * HEAD_DIM]
                    s2 = _dot(k_pair, qt_ref[0, g, hp])
                    s_all += [s2[:, :rg] + bias, s2[:, rg:] + bias]
                m_new, acc_new = [], []
                for i, h in enumerate(heads):
                    m_h = jnp.maximum(m_run[i], jnp.max(_fold8(s_all[i], jnp.maximum), axis=0, keepdims=True))
                    alpha = jnp.exp2(m_run[i] - m_h)
                    p = jnp.exp2(s_all[i] - m_h).astype(bf16)
                    vt_h = jnp.concatenate([vt_ref[0, h * HEAD_DIM:(h + 1) * HEAD_DIM, pl.ds(off, sup)], ones_rows],
                                           axis=0)
                    acc_new.append(alpha * acc[i] + _dot(vt_h, p))
                    m_new.append(m_h)
                return tuple(m_new), tuple(acc_new)

            _, acc = lax.fori_loop(0, n_sup, key_step,
                                   (tuple(jnp.full((1, rg), NEG_BIG, f32) for _ in heads),
                                    tuple(jnp.zeros((HEAD_DIM + 16, rg), f32) for _ in heads)))
            for i, h in enumerate(heads):
                o = acc[i][:HEAD_DIM] / acc[i][HEAD_DIM:HEAD_DIM + 1]
                inv = lax.rsqrt(jnp.mean(o * o, axis=0, keepdims=True) + EPS)
                outs.append(o * inv * nw_ref[h * HEAD_DIM:(h + 1) * HEAD_DIM, :])
        o_ref[0, pl.ds(goff, rg), :] = jnp.concatenate(outs, axis=0).T.astype(bf16)
        return carry

    lax.fori_loop(0, n_rg, attend, 0)


def _dsa(qt, qit, k, vt, kidx, wt, nw):
    b, s, _ = k.shape
    n_rb = s // QB
    n_sel = min(TOPK, s // 4)
    rg = qt.shape[-1] // 2
    s_pad = -(-s // rg) * rg
    kernel = functools.partial(_dsa_kernel, n_sel=n_sel)

    def per_batch(a):
        return pl.BlockSpec((1,) + a.shape[1:], lambda i: (i,) + (0,) * (a.ndim - 1))

    nw_col = nw.reshape(ATT_WIDTH, 1)
    return pl.pallas_call(
        kernel,
        out_shape=jax.ShapeDtypeStruct((b, s, ATT_WIDTH), bf16),
        grid=(b,),
        in_specs=[per_batch(qit), per_batch(kidx), per_batch(wt), per_batch(qt), per_batch(k), per_batch(vt),
                  pl.BlockSpec((ATT_WIDTH, 1), lambda i: (0, 0))],
        out_specs=pl.BlockSpec((1, s, ATT_WIDTH), lambda i: (i, 0, 0)),
        scratch_shapes=[pltpu.VMEM((s_pad, s), f32),
                        pltpu.VMEM((n_rb, 8, QB), jnp.int32),
                        pltpu.VMEM((n_rb, 8, QB), jnp.int32)],
        compiler_params=pltpu.CompilerParams(dimension_semantics=("arbitrary",),
                                             vmem_limit_bytes=VMEM_LIMIT),
        name="dsa_attention",
    )(qit, kidx, wt, qt, k, vt, nw_col)


ML_STATE_ROWS = ML_DIM + 16


def _mlstm_kernel(qm_ref, km_ref, vm_ref, om_ref, gt_ref, nw_ref, hm_ref, c_ref, cum_ref, gcol_ref, *, chunk):
    s_len = qm_ref.shape[1]
    n_chunks = s_len // chunk
    s_i = lax.broadcasted_iota(jnp.int32, (chunk, chunk), 0)
    t_i = lax.broadcasted_iota(jnp.int32, (chunk, chunk), 1)
    causal = s_i <= t_i
    tri_t = jnp.where(causal, 1.0, 0.0).astype(f32)
    first_row = lax.broadcasted_iota(jnp.int32, (ML_STATE_ROWS - ML_DIM, chunk), 0) == 0
    c_ref[...] = jnp.zeros(c_ref.shape, f32)

    n_g = 2 * ML_HEADS
    g_all = jnp.concatenate([gt_ref[0, :, c * chunk:(c + 1) * chunk] for c in range(n_chunks)], axis=0)
    cum_all = _dot(g_all, tri_t, precision=lax.Precision.HIGHEST)
    cum_ref[...] = cum_all
    for c in range(n_chunks):
        g_src = g_all[c * n_g:c * n_g + ML_HEADS] - cum_all[c * n_g + ML_HEADS:(c + 1) * n_g]
        gcol_ref[c] = jnp.concatenate([g_src, jnp.zeros((LANE - ML_HEADS, chunk), f32)], axis=0).T

    def body(c, m_state):
        off = pl.multiple_of(c * chunk, chunk)
        g_row = gt_ref[0, :, pl.ds(off, chunk)]
        cum_row = cum_ref[pl.ds(pl.multiple_of(c * n_g, n_g), n_g), :]
        g_src_col = gcol_ref[c]
        m_next = []
        for h in range(ML_HEADS):
            ig_row = g_row[h:h + 1, :]
            b_row = cum_row[ML_HEADS + h:ML_HEADS + h + 1, :]
            m_s = m_state[h]
            total = b_row[:, chunk - 1:chunk]
            hs = slice(h * ML_DIM, (h + 1) * ML_DIM)
            q_t = qm_ref[0, pl.ds(off, chunk), hs].astype(f32).T.astype(bf16)
            k_h = km_ref[0, pl.ds(off, chunk), hs]
            v_tf = vm_ref[0, pl.ds(off, chunk), hs].astype(f32).T

            d_t = jnp.where(causal, g_src_col[:, h:h + 1] + b_row, -jnp.inf)
            inter = b_row + m_s
            m_row = jnp.maximum(inter, jnp.max(d_t, axis=0, keepdims=True))
            w_inter = jnp.exp(inter - m_row)
            s_t = _dot(k_h, q_t) * jnp.exp(d_t - m_row)
            c_ext = c_ref[h]
            qc = _dot(c_ext.astype(bf16), q_t)
            num = _dot(v_tf.astype(bf16), s_t.astype(bf16)) + w_inter * qc[:ML_DIM]
            den = jnp.sum(s_t, axis=0, keepdims=True) + w_inter * qc[ML_DIM:ML_DIM + 1]
            hh = num / jnp.maximum(jnp.abs(den), jnp.exp(-m_row))

            w_log = total - b_row + ig_row
            m_new = jnp.maximum(total + m_s, jnp.max(w_log, axis=1, keepdims=True))
            decay = jnp.exp(total + m_s - m_new)
            w_state = jnp.exp(w_log - m_new)
            lhs = jnp.concatenate([(v_tf * w_state).astype(bf16),
                                   jnp.where(first_row, w_state, 0.0).astype(bf16)], axis=0)
            c_ref[h] = decay * c_ext + _dot(lhs, k_h)
            m_next.append(m_new)

            inv = lax.rsqrt(jnp.mean(hh * hh, axis=0, keepdims=True) + EPS)
            gate = jax.nn.sigmoid(om_ref[0, pl.ds(off, chunk), hs].T)
            out_t = gate * hh * inv * nw_ref[hs, :]
            hm_ref[0, pl.ds(off, chunk), hs] = out_t.T.astype(bf16)
        return tuple(m_next)

    unroll = next(u for u in (4, 2, 1) if n_chunks % u == 0)
    lax.fori_loop(0, n_chunks, body, tuple(jnp.zeros((1, 1), f32) for _ in range(ML_HEADS)), unroll=unroll)


def _mlstm(qm, km, vm, om, gt, nw):
    b, s, _ = qm.shape
    chunk = min(ML_CHUNK, s)
    kernel = functools.partial(_mlstm_kernel, chunk=chunk)
    tok = pl.BlockSpec((1, s, ML_WIDTH), lambda i: (i, 0, 0))
    return pl.pallas_call(
        kernel,
        out_shape=jax.ShapeDtypeStruct((b, s, ML_WIDTH), bf16),
        grid=(b,),
        in_specs=[tok, tok, tok, tok,
                  pl.BlockSpec((1, 2 * ML_HEADS, s), lambda i: (i, 0, 0)),
                  pl.BlockSpec((ML_WIDTH, 1), lambda i: (0, 0))],
        out_specs=tok,
        scratch_shapes=[pltpu.VMEM((ML_HEADS, ML_STATE_ROWS, ML_DIM), f32),
                        pltpu.VMEM((s // chunk * 2 * ML_HEADS, chunk), f32),
                        pltpu.VMEM((s // chunk, chunk, LANE), f32)],
        compiler_params=pltpu.CompilerParams(dimension_semantics=("arbitrary",),
                                             vmem_limit_bytes=VMEM_LIMIT),
        name="mlstm",
    )(qm, km, vm, om, gt, nw.reshape(ML_WIDTH, 1))


def _out_ffn_kernel(att_ref, hm_ref, x_ref, mod_ref, wout_ref, npost_ref, fpre_ref, wgu_ref, wdn_ref, fpost_ref,
                    o_ref, *, bounds):
    hid = wdn_ref.shape[0]
    g_m = mod_ref[0, 2:3, :]
    sh_f = mod_ref[0, 3:4, :]
    sc_f = mod_ref[0, 4:5, :]
    g_f = mod_ref[0, 5:6, :]
    tm = x_ref.shape[1]
    piece = tm // FFN_ROW_PIECES
    for r0 in range(0, tm, piece):
        rows = slice(r0, r0 + piece)
        y = _dot(att_ref[0, rows, :], wout_ref[0:ATT_WIDTH, :]) + _dot(hm_ref[0, rows, :], wout_ref[ATT_WIDTH:, :])
        x1 = x_ref[0, rows, :] + g_m * _rms(y, npost_ref[...])
        h2 = (_rms(x1, fpre_ref[...]) * (1.0 + sc_f) + sh_f).astype(bf16)
        y2 = jnp.zeros(x1.shape, f32)
        for c0, c1 in zip(bounds[:-1], bounds[1:]):
            g = _dot(h2, wgu_ref[:, c0:c1])
            u = _dot(h2, wgu_ref[:, hid + c0:hid + c1])
            a = (g * jax.nn.sigmoid(g) * u).astype(bf16)
            y2 = y2 + _dot(a, wdn_ref[c0:c1, :])
        o_ref[0, rows, :] = x1 + g_f * _rms(y2, fpost_ref[...])


def _out_ffn(att, hm, x, mod_l, wout, npost, fpre, wgu, wdn, fpost, tm, layer):
    b, s, d = x.shape
    hid = wdn.shape[1]
    mxu_tile = 256
    cut = -(-(hid // 2) // mxu_tile) * mxu_tile
    bounds = (0, cut, hid) if 0 < cut < hid else (0, hid)

    def tok(w):
        return pl.BlockSpec((1, tm, w), lambda i, j: (i, j, 0))

    def const(a):
        return pl.BlockSpec(a.shape, lambda i, j: (0,) * a.ndim, pipeline_mode=pl.Buffered(1))

    def of_layer(a):
        return pl.BlockSpec((None,) + a.shape[1:], lambda i, j: (layer,) + (0,) * (a.ndim - 1),
                            pipeline_mode=pl.Buffered(1))

    kernel = functools.partial(_out_ffn_kernel, bounds=bounds)
    return pl.pallas_call(
        kernel,
        out_shape=jax.ShapeDtypeStruct((b, s, d), f32),
        grid=(b, s // tm),
        in_specs=[tok(ATT_WIDTH), tok(ML_WIDTH), tok(d), pl.BlockSpec((1, 6, d), lambda i, j: (i, 0, 0)),
                  of_layer(wout), const(npost), const(fpre), of_layer(wgu), of_layer(wdn), const(fpost)],
        out_specs=tok(d),
        compiler_params=pltpu.CompilerParams(dimension_semantics=("arbitrary", "arbitrary"),
                                             vmem_limit_bytes=VMEM_LIMIT),
        name="out_ffn",
    )(att, hm, x, mod_l, wout, npost, fpre, wgu, wdn, fpost)


def _arrange_w_in(w):
    pts = [int(p) for p in np.cumsum(IN_SPLITS)[:-1]]
    cq, k_att, v_att, k_idx, w_idx, q_m, k_m, v_m, o_m, i_m, f_m = jnp.split(w, pts, axis=-1)
    misc = jnp.concatenate([k_idx, w_idx, i_m, f_m, jnp.zeros(w.shape[:-1] + (LANE - MISC_END,), w.dtype)], axis=-1)
    return jnp.concatenate([cq, k_att, v_att, misc, q_m, k_m, v_m, o_m], axis=-1).astype(bf16)


def kernel(x, c, positions, w_mod, b_mod, mix_norm_pre, mix_norm_post, w_in, q_latent_norm, w_q_up, w_qidx_up,
           conv_w, conv_b, b_igate, b_fgate, attn_out_norm, mlstm_out_norm, w_out, ffn_norm_pre, ffn_norm_post,
           w_gate_up, w_down):
    depth = w_mod.shape[0]
    b, s, d = x.shape
    tm = min(512, s)
    mod = _modulation(c, w_mod, b_mod).reshape(depth, b, 6, d)
    cos, sin = _rope_tables(positions)

    def row(a):
        return a.reshape(1, -1)

    win = _arrange_w_in(w_in)
    wq = jnp.concatenate([w_q_up, w_qidx_up], axis=-1).astype(bf16)
    wout, wgu, wdn = w_out.astype(bf16), w_gate_up.astype(bf16), w_down.astype(bf16)
    for l in range(depth):
        gb = jnp.zeros((1, LANE), f32).at[0, MISC_I:MISC_F].set(b_igate[l]).at[0, MISC_F:MISC_END].set(b_fgate[l])
        qt, qit, k, vt, kidx, wt, gt, qm, km, vm, om = _in_proj(
            x, mod[l], row(mix_norm_pre[l]), win, row(q_latent_norm[l]), wq, conv_w[l], row(conv_b[l]), gb,
            cos, sin, tm, l)
        att = _dsa(qt, qit, k, vt, kidx, wt, attn_out_norm[l])
        hm = _mlstm(qm, km, vm, om, gt, mlstm_out_norm[l])
        x = _out_ffn(att, hm, x, mod[l], wout, row(mix_norm_post[l]), row(ffn_norm_pre[l]), wgu, wdn,
                     row(ffn_norm_post[l]), tm, l)
    return x
```

```python
import functools

import numpy as np
import jax
import jax.numpy as jnp
from jax import lax
from jax.experimental import pallas as pl
from jax.experimental.pallas import tpu as pltpu

EPS = 1e-6
ROPE_THETA = 10000.0
ATT_HEADS = 8
HEAD_DIM = 64
ATT_WIDTH = ATT_HEADS * HEAD_DIM
Q_RANK = 256
IDX_HEADS = 8
ML_HEADS = 4
ML_DIM = 128
ML_WIDTH = ML_HEADS * ML_DIM
CONV_W = 4
TOPK = 256
QB = 128
ATT_HEADS_PER_LOOP = 8
ATT_KEY_SPLIT = 1
FFN_ROW_PIECES = 1
LANE = 128
ML_CHUNK = 256
IN_SPLITS = (Q_RANK, ATT_WIDTH, ATT_WIDTH, HEAD_DIM, IDX_HEADS,
             ML_WIDTH, ML_WIDTH, ML_WIDTH, ML_WIDTH, ML_HEADS, ML_HEADS)

OFF_CQ = 0
OFF_K = OFF_CQ + Q_RANK
OFF_V = OFF_K + ATT_WIDTH
OFF_MISC = OFF_V + ATT_WIDTH
OFF_QKM = OFF_MISC + LANE
OFF_VM = OFF_QKM + 2 * ML_WIDTH
OFF_OM = OFF_VM + ML_WIDTH
IN_COLS_PADDED = OFF_OM + ML_WIDTH
MISC_W = HEAD_DIM
MISC_I = MISC_W + IDX_HEADS
MISC_F = MISC_I + ML_HEADS
MISC_END = MISC_F + ML_HEADS

VMEM_LIMIT = 56 * 1024 * 1024
NEG_BIG = -1e30
LOG2E = float(np.log2(np.e))
KEY_POS_INF = 0x7F800000
KEY_NEG_INF = (0xFF800000 - (1 << 32)) ^ 0x7FFFFFFF

bf16 = jnp.bfloat16
f32 = jnp.float32


def _rms(x, w):
    return x * lax.rsqrt(jnp.mean(x * x, axis=-1, keepdims=True) + EPS) * w


def _dot(a, b, precision=None):
    return jnp.dot(a, b, preferred_element_type=f32, precision=precision)


def _rope(x, cos, sin_signed):
    n = x.shape[1]
    lane = lax.broadcasted_iota(jnp.int32, x.shape, 1)
    low = (lane & (HEAD_DIM // 2)) == 0
    rot = jnp.where(low, pltpu.roll(x, n - HEAD_DIM // 2, 1), pltpu.roll(x, HEAD_DIM // 2, 1))
    return x * cos + rot * sin_signed


def _mod_kernel(c_ref, w_ref, b_ref, o_ref):
    c = c_ref[...]
    c_act = c * jax.nn.sigmoid(c)
    o_ref[0] = _dot(c_act, w_ref[0], precision=lax.Precision.HIGHEST) + b_ref[0]


def _modulation(c, w_mod, b_mod):
    depth, d, d6 = w_mod.shape
    b = c.shape[0]
    nj = d6 // d
    return pl.pallas_call(
        _mod_kernel,
        out_shape=jax.ShapeDtypeStruct((depth, b, d6), f32),
        grid=(depth, nj),
        in_specs=[pl.BlockSpec((b, d), lambda l, j: (0, 0)),
                  pl.BlockSpec((1, d, d), lambda l, j: (l, 0, j)),
                  pl.BlockSpec((1, 1, d), lambda l, j: (l, 0, j))],
        out_specs=pl.BlockSpec((1, b, d), lambda l, j: (l, 0, j)),
        name="modulation",
    )(c, w_mod, b_mod.reshape(depth, 1, d6))


def _rope_table_kernel(pos_ref, invf_ref, sign_ref, cos_ref, sin_ref):
    ang = invf_ref[...] * pos_ref[0].astype(f32)
    reps = LANE // (HEAD_DIM // 2)
    cos_ref[0] = jnp.concatenate([jnp.cos(ang)] * reps, axis=0).T
    sin_ref[0] = jnp.concatenate([jnp.sin(ang)] * reps, axis=0).T * sign_ref[...]


def _rope_tables(positions):
    b, s = positions.shape
    ts = min(s, 512)
    inv_freq = ROPE_THETA ** (-jnp.arange(0, HEAD_DIM, 2, dtype=f32) / HEAD_DIM)
    sign = np.where((np.arange(LANE) % HEAD_DIM) < HEAD_DIM // 2, -1.0, 1.0).astype(np.float32).reshape(1, LANE)
    tab = jax.ShapeDtypeStruct((b, s, LANE), f32)
    return pl.pallas_call(
        _rope_table_kernel,
        out_shape=(tab, tab),
        grid=(b, s // ts),
        in_specs=[pl.BlockSpec((1, 1, ts), lambda i, j: (i, 0, j)),
                  pl.BlockSpec((HEAD_DIM // 2, 1), lambda i, j: (0, 0)),
                  pl.BlockSpec((1, LANE), lambda i, j: (0, 0))],
        out_specs=(pl.BlockSpec((1, ts, LANE), lambda i, j: (i, j, 0)),
                   pl.BlockSpec((1, ts, LANE), lambda i, j: (i, j, 0))),
        name="rope_tables",
    )(positions.reshape(b, 1, s), inv_freq.reshape(HEAD_DIM // 2, 1), jnp.asarray(sign))


def _in_proj_kernel(x_ref, mod_ref, nw_ref, win_ref, qln_ref, wq_ref, cw_ref, cb_ref, gb_ref, cos_ref, sin_ref,
                    qt_ref, qit_ref, k_ref, vt_ref, kidx_ref, wt_ref, gt_ref, qm_ref, km_ref, vm_ref, om_ref,
                    buf_ref):
    tm = x_ref.shape[1]

    @pl.when(pl.program_id(1) == 0)
    def _():
        buf_ref[0:8, :] = jnp.zeros((8, 2 * ML_WIDTH), f32)

    x = x_ref[0]
    sh = mod_ref[0, 0:1, :]
    sc = mod_ref[0, 1:2, :]
    hb = (_rms(x, nw_ref[...]) * (1.0 + sc) + sh).astype(bf16)

    cos = cos_ref[0]
    sin = sin_ref[0]
    reps = ATT_WIDTH // LANE
    cos_w = jnp.concatenate([cos] * reps, axis=1)
    sin_w = jnp.concatenate([sin] * reps, axis=1)

    cq = _dot(hb, win_ref[:, OFF_CQ:OFF_CQ + Q_RANK])
    cqn = _rms(cq, qln_ref[...]).astype(bf16)
    qq = _dot(cqn, wq_ref[...])
    q_t = (_rope(qq[:, :ATT_WIDTH], cos_w, sin_w) * (HEAD_DIM ** -0.5 * LOG2E)).T.astype(bf16)
    qi_t = _rope(qq[:, ATT_WIDTH:], cos_w, sin_w).T.astype(bf16)
    qt_ref[...] = jnp.zeros(qt_ref.shape, bf16)
    for h in range(ATT_HEADS):
        d0 = (h % 2) * HEAD_DIM
        qt_ref[0, 0, h // 2, d0:d0 + HEAD_DIM, (h % 2) * tm:(h % 2 + 1) * tm] = q_t[h * HEAD_DIM:(h + 1) * HEAD_DIM, :]
    for h in range(IDX_HEADS):
        qit_ref[0, 0, :, h * tm:(h + 1) * tm] = qi_t[h * HEAD_DIM:(h + 1) * HEAD_DIM, :]
    k_ref[0] = _rope(_dot(hb, win_ref[:, OFF_K:OFF_K + ATT_WIDTH]), cos_w, sin_w).astype(bf16)
    vt_ref[0] = _dot(hb, win_ref[:, OFF_V:OFF_V + ATT_WIDTH]).T.astype(bf16)

    m = _dot(hb, win_ref[:, OFF_MISC:OFF_MISC + LANE])
    lane = lax.broadcasted_iota(jnp.int32, m.shape, 1)
    roped = _rope(m, cos, sin)
    biased = m + gb_ref[...]
    logsig = jnp.minimum(biased, 0.0) - jnp.log1p(jnp.exp(-jnp.abs(biased)))
    w_scale = (IDX_HEADS * HEAD_DIM) ** -0.5
    misc = jnp.where(lane < MISC_W, roped,
                     jnp.where(lane < MISC_I, m * w_scale,
                               jnp.where(lane < MISC_F, biased,
                                         jnp.where(lane < MISC_END, logsig, 0.0))))
    misc_t = misc.T
    kidx_ref[0] = misc[:, 0:HEAD_DIM].astype(bf16)
    wt_ref[0] = misc_t[MISC_W:MISC_I, :]
    gt_ref[0] = misc_t[MISC_I:MISC_END, :]

    pre = _dot(hb, win_ref[:, OFF_QKM:OFF_QKM + 2 * ML_WIDTH])
    buf_ref[8:8 + tm, :] = pre
    acc = jnp.broadcast_to(cb_ref[...], pre.shape)
    for t in range(CONV_W):
        acc = acc + cw_ref[t:t + 1, :] * buf_ref[8 - (CONV_W - 1) + t:8 - (CONV_W - 1) + t + tm, :]
    buf_ref[0:8, :] = buf_ref[tm:tm + 8, :]
    qk = acc * jax.nn.sigmoid(acc)
    qm_ref[0] = (qk[:, :ML_WIDTH] * (ML_DIM ** -0.5)).astype(bf16)
    km_ref[0] = qk[:, ML_WIDTH:].astype(bf16)
    vm_ref[0] = _dot(hb, win_ref[:, OFF_VM:OFF_VM + ML_WIDTH]).astype(bf16)
    om_ref[0] = _dot(hb, win_ref[:, OFF_OM:OFF_OM + ML_WIDTH])


def _in_proj(x, mod_l, nw, win, qln, wq, cw, cb, gb, cos, sin, tm, layer):
    b, s, d = x.shape
    nt = s // tm

    def tok(w):
        return pl.BlockSpec((1, tm, w), lambda i, j: (i, j, 0))

    def tok_t(r):
        return pl.BlockSpec((1, r, tm), lambda i, j: (i, 0, j))

    def full(a):
        return pl.BlockSpec(a.shape, lambda i, j: (0,) * a.ndim)

    def of_layer(a):
        return pl.BlockSpec((None,) + a.shape[1:], lambda i, j: (layer,) + (0,) * (a.ndim - 1))

    sd = jax.ShapeDtypeStruct
    qt_shape = (ATT_HEADS // 2, 2 * HEAD_DIM, 2 * tm)
    qit_shape = (HEAD_DIM, IDX_HEADS * tm)
    out_shape = (sd((b, nt) + qt_shape, bf16), sd((b, nt) + qit_shape, bf16), sd((b, s, ATT_WIDTH), bf16),
                 sd((b, ATT_WIDTH, s), bf16), sd((b, s, HEAD_DIM), bf16), sd((b, IDX_HEADS, s), f32),
                 sd((b, 2 * ML_HEADS, s), f32), sd((b, s, ML_WIDTH), bf16),
                 sd((b, s, ML_WIDTH), bf16), sd((b, s, ML_WIDTH), bf16), sd((b, s, ML_WIDTH), f32))
    out_specs = (pl.BlockSpec((1, 1) + qt_shape, lambda i, j: (i, j, 0, 0, 0)),
                 pl.BlockSpec((1, 1) + qit_shape, lambda i, j: (i, j, 0, 0)),
                 tok(ATT_WIDTH), tok_t(ATT_WIDTH), tok(HEAD_DIM), tok_t(IDX_HEADS),
                 tok_t(2 * ML_HEADS), tok(ML_WIDTH), tok(ML_WIDTH), tok(ML_WIDTH), tok(ML_WIDTH))
    return pl.pallas_call(
        _in_proj_kernel,
        out_shape=out_shape,
        grid=(b, nt),
        in_specs=[tok(d), pl.BlockSpec((1, 6, d), lambda i, j: (i, 0, 0)), full(nw), of_layer(win), full(qln),
                  of_layer(wq),
                  full(cw), full(cb), full(gb), tok(LANE), tok(LANE)],
        out_specs=out_specs,
        scratch_shapes=[pltpu.VMEM((tm + 8, 2 * ML_WIDTH), f32)],
        compiler_params=pltpu.CompilerParams(dimension_semantics=("arbitrary", "arbitrary"),
                                             vmem_limit_bytes=VMEM_LIMIT),
        name="in_proj",
    )(x, mod_l, nw, win, qln, wq, cw, cb, gb, cos, sin)


def _key_to_f32(key):
    bits = jnp.where(key >= 0, key, key ^ jnp.int32(0x7FFFFFFF))
    return lax.bitcast_convert_type(bits, f32)


def _fold8(x, op):
    parts = [x[i * 8:(i + 1) * 8] for i in range(x.shape[0] // 8)]
    n_acc = min(4, len(parts))
    accs = parts[:n_acc]
    for i, part in enumerate(parts[n_acc:]):
        accs[i % n_acc] = op(accs[i % n_acc], part)
    while len(accs) > 1:
        accs = [op(accs[i], accs[i + 1]) for i in range(0, len(accs), 2)]
    return accs[0]


def _dsa_kernel(qit_ref, kidx_ref, wt_ref, qt_ref, k_ref, vt_ref, nw_ref, o_ref, sc_ref, lo_ref, hi_ref, *, n_sel):
    s_len = k_ref.shape[1]
    n_rb = s_len // QB
    rg = qt_ref.shape[-1] // 2
    n_rg = s_len // rg
    sup = rg
    kf = float(n_sel)
    key_i = lax.broadcasted_iota(jnp.int32, (QB, QB), 0)
    row_i = lax.broadcasted_iota(jnp.int32, (QB, QB), 1)
    key_g = lax.broadcasted_iota(jnp.int32, (QB, rg), 0)
    row_g = lax.broadcasted_iota(jnp.int32, (QB, rg), 1)

    def col_sum(x):
        return jnp.sum(x, axis=0, keepdims=True)

    def score_rows(g, carry):
        goff = pl.multiple_of(g * rg, rg)
        w = wt_ref[0, :, pl.ds(goff, rg)]

        def score_keys(c, carry2):
            for p in range(sup // QB):
                off = pl.multiple_of(c * sup + p * QB, QB)
                lg = _dot(kidx_ref[0, pl.ds(off, QB), :], qit_ref[0, g])
                acc = jnp.zeros((QB, rg), f32)
                for h in range(IDX_HEADS):
                    acc = acc + w[h:h + 1, :] * jnp.maximum(lg[:, h * rg:(h + 1) * rg], 0.0)
                sc_ref[pl.ds(off, QB), pl.ds(goff, rg)] = jnp.where(off + key_g <= goff + row_g, acc, -jnp.inf)
            return carry2

        lax.fori_loop(0, g + 1, score_keys, 0)
        return carry

    lax.fori_loop(0, n_rg, score_rows, 0)

    first = n_sel // QB
    ones_lhs = jnp.ones((16, s_len), bf16)

    def floor_avg(a, b):
        return (a & b) + ((a ^ b) >> 1)

    if (rg // QB) % 2 == 0:
        units = [(first, 1)] if first % 2 else []
        units += [(r, 2) for r in range(first + first % 2, n_rb - 1, 2)]
        if (n_rb - first - first % 2) % 2:
            units.append((n_rb - 1, 1))
    else:
        units = [(r, 1) for r in range(first, n_rb)]

    def search_pass(_, carry):
        for r, width in units:
            blocks = range(r, r + width)
            lo = jnp.concatenate([lo_ref[b, 0:1, :] for b in blocks], axis=1)
            hi = jnp.concatenate([hi_ref[b, 0:1, :] for b in blocks], axis=1)
            mid = floor_avg(lo, hi)
            n_keys = (r + width) * QB
            hit = jnp.where(sc_ref[0:n_keys, r * QB:(r + width) * QB] >= _key_to_f32(mid), 1.0, 0.0).astype(bf16)
            ge = _dot(ones_lhs[:, :n_keys], hit)[0:1] >= kf
            new_lo = jnp.where(ge, mid, lo)
            new_hi = jnp.where(ge, hi, mid)
            for i, b in enumerate(blocks):
                lo_ref[b, 0:1, :] = new_lo[:, i * QB:(i + 1) * QB]
                hi_ref[b, 0:1, :] = new_hi[:, i * QB:(i + 1) * QB]
        return carry

    lo_ref[...] = jnp.full(lo_ref.shape, KEY_NEG_INF, jnp.int32)
    hi_ref[...] = jnp.full(hi_ref.shape, KEY_POS_INF, jnp.int32)
    lax.fori_loop(0, 32, search_pass, 0)

    def finalize(r, carry):
        roff = pl.multiple_of(r * QB, QB)
        thr = _key_to_f32(lo_ref[r, 0:1, :])

        def count(pred):
            def body(c, part):
                off = pl.multiple_of(c * QB, QB)
                blk = sc_ref[pl.ds(off, QB), pl.ds(roff, QB)]
                return part + jnp.where(pred(blk, off + key_i), 1.0, 0.0)
            return col_sum(lax.fori_loop(0, r + 1, body, jnp.zeros((QB, QB), f32)))

        finite = thr > -jnp.inf
        tied = jnp.logical_and(count(lambda blk, pos: blk >= thr) != kf, finite)
        cut0 = jnp.where(finite, s_len, -1)

        def break_ties(_):
            need = kf - count(lambda blk, pos: blk > thr)

            def bisect_idx(_, lh):
                lo_j, hi_j = lh
                mid = (lo_j + hi_j) >> 1
                ok = count(lambda blk, pos: jnp.logical_and(blk == thr, pos <= mid)) >= need
                return jnp.where(ok, lo_j, mid), jnp.where(ok, mid, hi_j)

            n_it = int(np.ceil(np.log2(s_len + 1)))
            _, hi_j = lax.fori_loop(0, n_it, bisect_idx, (jnp.full((1, QB), -1, jnp.int32),
                                                          jnp.full((1, QB), s_len - 1, jnp.int32)))
            return jnp.where(tied, hi_j, cut0)

        cut = lax.cond(jnp.max(tied.astype(jnp.int32)) > 0, break_ties, lambda _: cut0, 0)

        def write_mask(c, carry2):
            off = pl.multiple_of(c * QB, QB)
            blk = sc_ref[pl.ds(off, QB), pl.ds(roff, QB)]
            sel = jnp.logical_or(blk > thr, jnp.logical_and(blk == thr, off + key_i <= cut))
            sc_ref[pl.ds(off, QB), pl.ds(roff, QB)] = jnp.where(sel, 0.0, NEG_BIG)
            return carry2

        lax.fori_loop(0, per_g * (r // per_g + 1), write_mask, 0)
        return carry

    per_g = rg // QB

    any_tied = jnp.zeros((1, QB), jnp.int32)
    for r, width in units:
        n_keys = (r + width) * QB
        thr = jnp.concatenate([_key_to_f32(lo_ref[b, 0:1, :]) for b in range(r, r + width)], axis=1)
        hit = jnp.where(sc_ref[0:n_keys, r * QB:(r + width) * QB] >= thr, 1.0, 0.0).astype(bf16)
        over = (_dot(ones_lhs[:, :n_keys], hit)[0:1] != kf).astype(jnp.int32)
        for i in range(width):
            any_tied = jnp.maximum(any_tied, over[:, i * QB:(i + 1) * QB])

    def with_ties(_):
        lax.fori_loop(0, n_rb, finalize, 0)
        return 0

    def without_ties(_):
        for r in range(n_rb):
            thr = _key_to_f32(lo_ref[r, 0:1, :])
            thr = jnp.maximum(thr, float(jnp.finfo(f32).min))
            for c in range(per_g * (r // per_g + 1)):
                tile = (slice(c * QB, (c + 1) * QB), slice(r * QB, (r + 1) * QB))
                sc_ref[tile] = jnp.where(sc_ref[tile] >= thr, 0.0, NEG_BIG)
        return 0

    lax.cond(jnp.max(any_tied) > 0, with_ties, without_ties, 0)

    def attend(g, carry, sup=sup // ATT_KEY_SPLIT):
        goff = pl.multiple_of(g * rg, rg)
        n_sup = (g + 1) * ATT_KEY_SPLIT

        ones_rows = jnp.ones((16, sup), bf16)
        outs = []
        for h0 in range(0, ATT_HEADS, ATT_HEADS_PER_LOOP):
            heads = range(h0, h0 + ATT_HEADS_PER_LOOP)

            def key_step(c, carry2, heads=heads):
                m_run, acc = carry2
                off = pl.multiple_of(c * sup, sup)
                bias = sc_ref[pl.ds(off, sup), pl.ds(goff, rg)]
                s_all = []
                for hp in range(heads[0] // 2, heads[-1] // 2 + 1):
                    k_pair = k_ref[0, pl.ds(off, sup), hp * 2 * HEAD_DIM:(hp + 1) * 2 * HEAD_DIM]
                    s2 = _dot(k_pair, qt_ref[0, g, hp])
                    s_all += [s2[:, :rg] + bias, s2[:, rg:] + bias]
                m_new, acc_new = [], []
                for i, h in enumerate(heads):
                    m_h = jnp.maximum(m_run[i], jnp.max(_fold8(s_all[i], jnp.maximum), axis=0, keepdims=True))
                    alpha = jnp.exp2(m_run[i] - m_h)
                    p = jnp.exp2(s_all[i] - m_h).astype(bf16)
                    vt_h = jnp.concatenate([vt_ref[0, h * HEAD_DIM:(h + 1) * HEAD_DIM, pl.ds(off, sup)], ones_rows],
                                           axis=0)
                    acc_new.append(alpha * acc[i] + _dot(vt_h, p))
                    m_new.append(m_h)
                return tuple(m_new), tuple(acc_new)

            _, acc = lax.fori_loop(0, n_sup, key_step,
                                   (tuple(jnp.full((1, rg), NEG_BIG, f32) for _ in heads),
                                    tuple(jnp.zeros((HEAD_DIM + 16, rg), f32) for _ in heads)))
            for i, h in enumerate(heads):
                o = acc[i][:HEAD_DIM] / acc[i][HEAD_DIM:HEAD_DIM + 1]
                inv = lax.rsqrt(jnp.mean(o * o, axis=0, keepdims=True) + EPS)
                outs.append(o * inv * nw_ref[h * HEAD_DIM:(h + 1) * HEAD_DIM, :])
        o_ref[0, pl.ds(goff, rg), :] = jnp.concatenate(outs, axis=0).T.astype(bf16)
        return carry

    lax.fori_loop(0, n_rg, attend, 0)


def _dsa(qt, qit, k, vt, kidx, wt, nw):
    b, s, _ = k.shape
    n_rb = s // QB
    n_sel = min(TOPK, s // 4)
    rg = qt.shape[-1] // 2
    s_pad = -(-s // rg) * rg
    kernel = functools.partial(_dsa_kernel, n_sel=n_sel)

    def per_batch(a):
        return pl.BlockSpec((1,) + a.shape[1:], lambda i: (i,) + (0,) * (a.ndim - 1))

    nw_col = nw.reshape(ATT_WIDTH, 1)
    return pl.pallas_call(
        kernel,
        out_shape=jax.ShapeDtypeStruct((b, s, ATT_WIDTH), bf16),
        grid=(b,),
        in_specs=[per_batch(qit), per_batch(kidx), per_batch(wt), per_batch(qt), per_batch(k), per_batch(vt),
                  pl.BlockSpec((ATT_WIDTH, 1), lambda i: (0, 0))],
        out_specs=pl.BlockSpec((1, s, ATT_WIDTH), lambda i: (i, 0, 0)),
        scratch_shapes=[pltpu.VMEM((s_pad, s), f32),
                        pltpu.VMEM((n_rb, 8, QB), jnp.int32),
                        pltpu.VMEM((n_rb, 8, QB), jnp.int32)],
        compiler_params=pltpu.CompilerParams(dimension_semantics=("arbitrary",),
                                             vmem_limit_bytes=VMEM_LIMIT),
        name="dsa_attention",
    )(qit, kidx, wt, qt, k, vt, nw_col)


ML_STATE_ROWS = ML_DIM + 16


def _mlstm_kernel(qm_ref, km_ref, vm_ref, om_ref, gt_ref, nw_ref, hm_ref, c_ref, cum_ref, gcol_ref, *, chunk):
    s_len = qm_ref.shape[1]
    n_chunks = s_len // chunk
    s_i = lax.broadcasted_iota(jnp.int32, (chunk, chunk), 0)
    t_i = lax.broadcasted_iota(jnp.int32, (chunk, chunk), 1)
    causal = s_i <= t_i
    tri_t = jnp.where(causal, 1.0, 0.0).astype(f32)
    first_row = lax.broadcasted_iota(jnp.int32, (ML_STATE_ROWS - ML_DIM, chunk), 0) == 0
    c_ref[...] = jnp.zeros(c_ref.shape, f32)

    n_g = 2 * ML_HEADS
    g_all = jnp.concatenate([gt_ref[0, :, c * chunk:(c + 1) * chunk] for c in range(n_chunks)], axis=0)
    cum_all = _dot(g_all, tri_t, precision=lax.Precision.HIGHEST)
    cum_ref[...] = cum_all
    for c in range(n_chunks):
        g_src = g_all[c * n_g:c * n_g + ML_HEADS] - cum_all[c * n_g + ML_HEADS:(c + 1) * n_g]
        gcol_ref[c] = jnp.concatenate([g_src, jnp.zeros((LANE - ML_HEADS, chunk), f32)], axis=0).T

    def body(c, m_state):
        off = pl.multiple_of(c * chunk, chunk)
        g_row = gt_ref[0, :, pl.ds(off, chunk)]
        cum_row = cum_ref[pl.ds(pl.multiple_of(c * n_g, n_g), n_g), :]
        g_src_col = gcol_ref[c]
        m_next = []
        for h in range(ML_HEADS):
            ig_row = g_row[h:h + 1, :]
            b_row = cum_row[ML_HEADS + h:ML_HEADS + h + 1, :]
            m_s = m_state[h]
            total = b_row[:, chunk - 1:chunk]
            hs = slice(h * ML_DIM, (h + 1) * ML_DIM)
            q_t = qm_ref[0, pl.ds(off, chunk), hs].astype(f32).T.astype(bf16)
            k_h = km_ref[0, pl.ds(off, chunk), hs]
            v_tf = vm_ref[0, pl.ds(off, chunk), hs].astype(f32).T

            d_t = jnp.where(causal, g_src_col[:, h:h + 1] + b_row, -jnp.inf)
            inter = b_row + m_s
            m_row = jnp.maximum(inter, jnp.max(d_t, axis=0, keepdims=True))
            w_inter = jnp.exp(inter - m_row)
            s_t = _dot(k_h, q_t) * jnp.exp(d_t - m_row)
            c_ext = c_ref[h]
            qc = _dot(c_ext.astype(bf16), q_t)
            num = _dot(v_tf.astype(bf16), s_t.astype(bf16)) + w_inter * qc[:ML_DIM]
            den = jnp.sum(s_t, axis=0, keepdims=True) + w_inter * qc[ML_DIM:ML_DIM + 1]
            hh = num / jnp.maximum(jnp.abs(den), jnp.exp(-m_row))

            w_log = total - b_row + ig_row
            m_new = jnp.maximum(total + m_s, jnp.max(w_log, axis=1, keepdims=True))
            decay = jnp.exp(total + m_s - m_new)
            w_state = jnp.exp(w_log - m_new)
            lhs = jnp.concatenate([(v_tf * w_state).astype(bf16),
                                   jnp.where(first_row, w_state, 0.0).astype(bf16)], axis=0)
            c_ref[h] = decay * c_ext + _dot(lhs, k_h)
            m_next.append(m_new)

            inv = lax.rsqrt(jnp.mean(hh * hh, axis=0, keepdims=True) + EPS)
            gate = jax.nn.sigmoid(om_ref[0, pl.ds(off, chunk), hs].T)
            out_t = gate * hh * inv * nw_ref[hs, :]
            hm_ref[0, pl.ds(off, chunk), hs] = out_t.T.astype(bf16)
        return tuple(m_next)

    unroll = next(u for u in (4, 2, 1) if n_chunks % u == 0)
    lax.fori_loop(0, n_chunks, body, tuple(jnp.zeros((1, 1), f32) for _ in range(ML_HEADS)), unroll=unroll)


def _mlstm(qm, km, vm, om, gt, nw):
    b, s, _ = qm.shape
    chunk = min(ML_CHUNK, s)
    kernel = functools.partial(_mlstm_kernel, chunk=chunk)
    tok = pl.BlockSpec((1, s, ML_WIDTH), lambda i: (i, 0, 0))
    return pl.pallas_call(
        kernel,
        out_shape=jax.ShapeDtypeStruct((b, s, ML_WIDTH), bf16),
        grid=(b,),
        in_specs=[tok, tok, tok, tok,
                  pl.BlockSpec((1, 2 * ML_HEADS, s), lambda i: (i, 0, 0)),
                  pl.BlockSpec((ML_WIDTH, 1), lambda i: (0, 0))],
        out_specs=tok,
        scratch_shapes=[pltpu.VMEM((ML_HEADS, ML_STATE_ROWS, ML_DIM), f32),
                        pltpu.VMEM((s // chunk * 2 * ML_HEADS, chunk), f32),
                        pltpu.VMEM((s // chunk, chunk, LANE), f32)],
        compiler_params=pltpu.CompilerParams(dimension_semantics=("arbitrary",),
                                             vmem_limit_bytes=VMEM_LIMIT),
        name="mlstm",
    )(qm, km, vm, om, gt, nw.reshape(ML_WIDTH, 1))


def _out_ffn_kernel(att_ref, hm_ref, x_ref, mod_ref, wout_ref, npost_ref, fpre_ref, wgu_ref, wdn_ref, fpost_ref,
                    o_ref, *, bounds):
    hid = wdn_ref.shape[0]
    g_m = mod_ref[0, 2:3, :]
    sh_f = mod_ref[0, 3:4, :]
    sc_f = mod_ref[0, 4:5, :]
    g_f = mod_ref[0, 5:6, :]
    tm = x_ref.shape[1]
    piece = tm // FFN_ROW_PIECES
    for r0 in range(0, tm, piece):
        rows = slice(r0, r0 + piece)
        y = _dot(att_ref[0, rows, :], wout_ref[0:ATT_WIDTH, :]) + _dot(hm_ref[0, rows, :], wout_ref[ATT_WIDTH:, :])
        x1 = x_ref[0, rows, :] + g_m * _rms(y, npost_ref[...])
        h2 = (_rms(x1, fpre_ref[...]) * (1.0 + sc_f) + sh_f).astype(bf16)
        y2 = jnp.zeros(x1.shape, f32)
        for c0, c1 in zip(bounds[:-1], bounds[1:]):
            g = _dot(h2, wgu_ref[:, c0:c1])
            u = _dot(h2, wgu_ref[:, hid + c0:hid + c1])
            a = (g * jax.nn.sigmoid(g) * u).astype(bf16)
            y2 = y2 + _dot(a, wdn_ref[c0:c1, :])
        o_ref[0, rows, :] = x1 + g_f * _rms(y2, fpost_ref[...])


def _out_ffn(att, hm, x, mod_l, wout, npost, fpre, wgu, wdn, fpost, tm, layer):
    b, s, d = x.shape
    hid = wdn.shape[1]
    mxu_tile = 256
    cut = -(-(hid // 2) // mxu_tile) * mxu_tile
    bounds = (0, cut, hid) if 0 < cut < hid else (0, hid)

    def tok(w):
        return pl.BlockSpec((1, tm, w), lambda i, j: (i, j, 0))

    def const(a):
        return pl.BlockSpec(a.shape, lambda i, j: (0,) * a.ndim, pipeline_mode=pl.Buffered(1))

    def of_layer(a):
        return pl.BlockSpec((None,) + a.shape[1:], lambda i, j: (layer,) + (0,) * (a.ndim - 1),
                            pipeline_mode=pl.Buffered(1))

    kernel = functools.partial(_out_ffn_kernel, bounds=bounds)
    return pl.pallas_call(
        kernel,
        out_shape=jax.ShapeDtypeStruct((b, s, d), f32),
        grid=(b, s // tm),
        in_specs=[tok(ATT_WIDTH), tok(ML_WIDTH), tok(d), pl.BlockSpec((1, 6, d), lambda i, j: (i, 0, 0)),
                  of_layer(wout), const(npost), const(fpre), of_layer(wgu), of_layer(wdn), const(fpost)],
        out_specs=tok(d),
        compiler_params=pltpu.CompilerParams(dimension_semantics=("arbitrary", "arbitrary"),
                                             vmem_limit_bytes=VMEM_LIMIT),
        name="out_ffn",
    )(att, hm, x, mod_l, wout, npost, fpre, wgu, wdn, fpost)


def _arrange_w_in(w):
    pts = [int(p) for p in np.cumsum(IN_SPLITS)[:-1]]
    cq, k_att, v_att, k_idx, w_idx, q_m, k_m, v_m, o_m, i_m, f_m = jnp.split(w, pts, axis=-1)
    misc = jnp.concatenate([k_idx, w_idx, i_m, f_m, jnp.zeros(w.shape[:-1] + (LANE - MISC_END,), w.dtype)], axis=-1)
    return jnp.concatenate([cq, k_att, v_att, misc, q_m, k_m, v_m, o_m], axis=-1).astype(bf16)


def kernel(x, c, positions, w_mod, b_mod, mix_norm_pre, mix_norm_post, w_in, q_latent_norm, w_q_up, w_qidx_up,
           conv_w, conv_b, b_igate, b_fgate, attn_out_norm, mlstm_out_norm, w_out, ffn_norm_pre, ffn_norm_post,
           w_gate_up, w_down):
    depth = w_mod.shape[0]
    b, s, d = x.shape
    tm = min(512, s)
    mod = _modulation(c, w_mod, b_mod).reshape(depth, b, 6, d)
    cos, sin = _rope_tables(positions)

    def row(a):
        return a.reshape(1, -1)

    win = _arrange_w_in(w_in)
    wq = jnp.concatenate([w_q_up, w_qidx_up], axis=-1).astype(bf16)
    wout, wgu, wdn = w_out.astype(bf16), w_gate_up.astype(bf16), w_down.astype(bf16)
    for l in range(depth):
        gb = jnp.zeros((1, LANE), f32).at[0, MISC_I:MISC_F].set(b_igate[l]).at[0, MISC_F:MISC_END].set(b_fgate[l])
        qt, qit, k, vt, kidx, wt, gt, qm, km, vm, om = _in_proj(
            x, mod[l], row(mix_norm_pre[l]), win, row(q_latent_norm[l]), wq, conv_w[l], row(conv_b[l]), gb,
            cos, sin, tm, l)
        att = _dsa(qt, qit, k, vt, kidx, wt, attn_out_norm[l])
        hm = _mlstm(qm, km, vm, om, gt, mlstm_out_norm[l])
        x = _out_ffn(att, hm, x, mod[l], wout, row(mix_norm_post[l]), row(ffn_norm_pre[l]), wgu, wdn,
                     row(ffn_norm_post[l]), tm, l)
    return x
```

```python
import functools

import numpy as np
import jax
import jax.numpy as jnp
from jax import lax
from jax.experimental import pallas as pl
from jax.experimental.pallas import tpu as pltpu

EPS = 1e-6
ROPE_THETA = 10000.0
ATT_HEADS = 8
HEAD_DIM = 64
ATT_WIDTH = ATT_HEADS * HEAD_DIM
Q_RANK = 256
IDX_HEADS = 8
ML_HEADS = 4
ML_DIM = 128
ML_WIDTH = ML_HEADS * ML_DIM
CONV_W = 4
TOPK = 256
QB = 128
ATT_HEADS_PER_LOOP = 8
ATT_KEY_SPLIT = 1
FFN_ROW_PIECES = 1
LANE = 128
ML_CHUNK = 256
IN_SPLITS = (Q_RANK, ATT_WIDTH, ATT_WIDTH, HEAD_DIM, IDX_HEADS,
             ML_WIDTH, ML_WIDTH, ML_WIDTH, ML_WIDTH, ML_HEADS, ML_HEADS)

OFF_CQ = 0
OFF_K = OFF_CQ + Q_RANK
OFF_V = OFF_K + ATT_WIDTH
OFF_MISC = OFF_V + ATT_WIDTH
OFF_QKM = OFF_MISC + LANE
OFF_VM = OFF_QKM + 2 * ML_WIDTH
OFF_OM = OFF_VM + ML_WIDTH
IN_COLS_PADDED = OFF_OM + ML_WIDTH
MISC_W = HEAD_DIM
MISC_I = MISC_W + IDX_HEADS
MISC_F = MISC_I + ML_HEADS
MISC_END = MISC_F + ML_HEADS

VMEM_LIMIT = 56 * 1024 * 1024
NEG_BIG = -1e30
LOG2E = float(np.log2(np.e))
KEY_POS_INF = 0x7F800000
KEY_NEG_INF = (0xFF800000 - (1 << 32)) ^ 0x7FFFFFFF

bf16 = jnp.bfloat16
f32 = jnp.float32


def _rms(x, w):
    return x * lax.rsqrt(jnp.mean(x * x, axis=-1, keepdims=True) + EPS) * w


def _dot(a, b, precision=None):
    return jnp.dot(a, b, preferred_element_type=f32, precision=precision)


def _rope(x, cos, sin_signed):
    n = x.shape[1]
    lane = lax.broadcasted_iota(jnp.int32, x.shape, 1)
    low = (lane & (HEAD_DIM // 2)) == 0
    rot = jnp.where(low, pltpu.roll(x, n - HEAD_DIM // 2, 1), pltpu.roll(x, HEAD_DIM // 2, 1))
    return x * cos + rot * sin_signed


def _mod_kernel(c_ref, w_ref, b_ref, o_ref):
    c = c_ref[...]
    c_act = c * jax.nn.sigmoid(c)
    o_ref[0] = _dot(c_act, w_ref[0], precision=lax.Precision.HIGHEST) + b_ref[0]


def _modulation(c, w_mod, b_mod):
    depth, d, d6 = w_mod.shape
    b = c.shape[0]
    nj = d6 // d
    return pl.pallas_call(
        _mod_kernel,
        out_shape=jax.ShapeDtypeStruct((depth, b, d6), f32),
        grid=(depth, nj),
        in_specs=[pl.BlockSpec((b, d), lambda l, j: (0, 0)),
                  pl.BlockSpec((1, d, d), lambda l, j: (l, 0, j)),
                  pl.BlockSpec((1, 1, d), lambda l, j: (l, 0, j))],
        out_specs=pl.BlockSpec((1, b, d), lambda l, j: (l, 0, j)),
        name="modulation",
    )(c, w_mod, b_mod.reshape(depth, 1, d6))


def _rope_table_kernel(pos_ref, invf_ref, sign_ref, cos_ref, sin_ref):
    ang = invf_ref[...] * pos_ref[0].astype(f32)
    reps = LANE // (HEAD_DIM // 2)
    cos_ref[0] = jnp.concatenate([jnp.cos(ang)] * reps, axis=0).T
    sin_ref[0] = jnp.concatenate([jnp.sin(ang)] * reps, axis=0).T * sign_ref[...]


def _rope_tables(positions):
    b, s = positions.shape
    ts = min(s, 512)
    inv_freq = ROPE_THETA ** (-jnp.arange(0, HEAD_DIM, 2, dtype=f32) / HEAD_DIM)
    sign = np.where((np.arange(LANE) % HEAD_DIM) < HEAD_DIM // 2, -1.0, 1.0).astype(np.float32).reshape(1, LANE)
    tab = jax.ShapeDtypeStruct((b, s, LANE), f32)
    return pl.pallas_call(
        _rope_table_kernel,
        out_shape=(tab, tab),
        grid=(b, s // ts),
        in_specs=[pl.BlockSpec((1, 1, ts), lambda i, j: (i, 0, j)),
                  pl.BlockSpec((HEAD_DIM // 2, 1), lambda i, j: (0, 0)),
                  pl.BlockSpec((1, LANE), lambda i, j: (0, 0))],
        out_specs=(pl.BlockSpec((1, ts, LANE), lambda i, j: (i, j, 0)),
                   pl.BlockSpec((1, ts, LANE), lambda i, j: (i, j, 0))),
        name="rope_tables",
    )(positions.reshape(b, 1, s), inv_freq.reshape(HEAD_DIM // 2, 1), jnp.asarray(sign))


def _in_proj_kernel(x_ref, mod_ref, nw_ref, win_ref, qln_ref, wq_ref, cw_ref, cb_ref, gb_ref, cos_ref, sin_ref,
                    qt_ref, qit_ref, k_ref, vt_ref, kidx_ref, wt_ref, gt_ref, qm_ref, km_ref, vm_ref, om_ref,
                    buf_ref):
    tm = x_ref.shape[1]

    @pl.when(pl.program_id(1) == 0)
    def _():
        buf_ref[0:8, :] = jnp.zeros((8, 2 * ML_WIDTH), f32)

    x = x_ref[0]
    sh = mod_ref[0, 0:1, :]
    sc = mod_ref[0, 1:2, :]
    hb = (_rms(x, nw_ref[...]) * (1.0 + sc) + sh).astype(bf16)

    cos = cos_ref[0]
    sin = sin_ref[0]
    reps = ATT_WIDTH // LANE
    cos_w = jnp.concatenate([cos] * reps, axis=1)
    sin_w = jnp.concatenate([sin] * reps, axis=1)

    cq = _dot(hb, win_ref[:, OFF_CQ:OFF_CQ + Q_RANK])
    cqn = _rms(cq, qln_ref[...]).astype(bf16)
    qq = _dot(cqn, wq_ref[...])
    q_t = (_rope(qq[:, :ATT_WIDTH], cos_w, sin_w) * (HEAD_DIM ** -0.5 * LOG2E)).T.astype(bf16)
    qi_t = _rope(qq[:, ATT_WIDTH:], cos_w, sin_w).T.astype(bf16)
    qt_ref[...] = jnp.zeros(qt_ref.shape, bf16)
    for h in range(ATT_HEADS):
        d0 = (h % 2) * HEAD_DIM
        qt_ref[0, 0, h // 2, d0:d0 + HEAD_DIM, (h % 2) * tm:(h % 2 + 1) * tm] = q_t[h * HEAD_DIM:(h + 1) * HEAD_DIM, :]
    for h in range(IDX_HEADS):
        qit_ref[0, 0, :, h * tm:(h + 1) * tm] = qi_t[h * HEAD_DIM:(h + 1) * HEAD_DIM, :]
    k_ref[0] = _rope(_dot(hb, win_ref[:, OFF_K:OFF_K + ATT_WIDTH]), cos_w, sin_w).astype(bf16)
    vt_ref[0] = _dot(hb, win_ref[:, OFF_V:OFF_V + ATT_WIDTH]).T.astype(bf16)

    m = _dot(hb, win_ref[:, OFF_MISC:OFF_MISC + LANE])
    lane = lax.broadcasted_iota(jnp.int32, m.shape, 1)
    roped = _rope(m, cos, sin)
    biased = m + gb_ref[...]
    logsig = jnp.minimum(biased, 0.0) - jnp.log1p(jnp.exp(-jnp.abs(biased)))
    w_scale = (IDX_HEADS * HEAD_DIM) ** -0.5
    misc = jnp.where(lane < MISC_W, roped,
                     jnp.where(lane < MISC_I, m * w_scale,
                               jnp.where(lane < MISC_F, biased,
                                         jnp.where(lane < MISC_END, logsig, 0.0))))
    misc_t = misc.T
    kidx_ref[0] = misc[:, 0:HEAD_DIM].astype(bf16)
    wt_ref[0] = misc_t[MISC_W:MISC_I, :]
    gt_ref[0] = misc_t[MISC_I:MISC_END, :]

    pre = _dot(hb, win_ref[:, OFF_QKM:OFF_QKM + 2 * ML_WIDTH])
    buf_ref[8:8 + tm, :] = pre
    acc = jnp.broadcast_to(cb_ref[...], pre.shape)
    hist = buf_ref[...]
    for t in range(CONV_W):
        shift = CONV_W - 1 - t
        shifted = pltpu.roll(hist, shift, 0) if shift else hist
        acc = acc + cw_ref[t:t + 1, :] * shifted[8:8 + tm, :]
    buf_ref[0:8, :] = buf_ref[tm:tm + 8, :]
    qk = acc * jax.nn.sigmoid(acc)
    qm_ref[0] = (qk[:, :ML_WIDTH] * (ML_DIM ** -0.5)).astype(bf16)
    km_ref[0] = qk[:, ML_WIDTH:].astype(bf16)
    vm_ref[0] = _dot(hb, win_ref[:, OFF_VM:OFF_VM + ML_WIDTH]).astype(bf16)
    om_ref[0] = _dot(hb, win_ref[:, OFF_OM:OFF_OM + ML_WIDTH])


def _in_proj(x, mod_l, nw, win, qln, wq, cw, cb, gb, cos, sin, tm, layer):
    b, s, d = x.shape
    nt = s // tm

    def tok(w):
        return pl.BlockSpec((1, tm, w), lambda i, j: (i, j, 0))

    def tok_t(r):
        return pl.BlockSpec((1, r, tm), lambda i, j: (i, 0, j))

    def full(a):
        return pl.BlockSpec(a.shape, lambda i, j: (0,) * a.ndim)

    def of_layer(a):
        return pl.BlockSpec((None,) + a.shape[1:], lambda i, j: (layer,) + (0,) * (a.ndim - 1))

    sd = jax.ShapeDtypeStruct
    qt_shape = (ATT_HEADS // 2, 2 * HEAD_DIM, 2 * tm)
    qit_shape = (HEAD_DIM, IDX_HEADS * tm)
    out_shape = (sd((b, nt) + qt_shape, bf16), sd((b, nt) + qit_shape, bf16), sd((b, s, ATT_WIDTH), bf16),
                 sd((b, ATT_WIDTH, s), bf16), sd((b, s, HEAD_DIM), bf16), sd((b, IDX_HEADS, s), f32),
                 sd((b, 2 * ML_HEADS, s), f32), sd((b, s, ML_WIDTH), bf16),
                 sd((b, s, ML_WIDTH), bf16), sd((b, s, ML_WIDTH), bf16), sd((b, s, ML_WIDTH), f32))
    out_specs = (pl.BlockSpec((1, 1) + qt_shape, lambda i, j: (i, j, 0, 0, 0)),
                 pl.BlockSpec((1, 1) + qit_shape, lambda i, j: (i, j, 0, 0)),
                 tok(ATT_WIDTH), tok_t(ATT_WIDTH), tok(HEAD_DIM), tok_t(IDX_HEADS),
                 tok_t(2 * ML_HEADS), tok(ML_WIDTH), tok(ML_WIDTH), tok(ML_WIDTH), tok(ML_WIDTH))
    return pl.pallas_call(
        _in_proj_kernel,
        out_shape=out_shape,
        grid=(b, nt),
        in_specs=[tok(d), pl.BlockSpec((1, 6, d), lambda i, j: (i, 0, 0)), full(nw), of_layer(win), full(qln),
                  of_layer(wq),
                  full(cw), full(cb), full(gb), tok(LANE), tok(LANE)],
        out_specs=out_specs,
        scratch_shapes=[pltpu.VMEM((tm + 8, 2 * ML_WIDTH), f32)],
        compiler_params=pltpu.CompilerParams(dimension_semantics=("arbitrary", "arbitrary"),
                                             vmem_limit_bytes=VMEM_LIMIT),
        name="in_proj",
    )(x, mod_l, nw, win, qln, wq, cw, cb, gb, cos, sin)


def _key_to_f32(key):
    bits = jnp.where(key >= 0, key, key ^ jnp.int32(0x7FFFFFFF))
    return lax.bitcast_convert_type(bits, f32)


def _fold8(x, op):
    parts = [x[i * 8:(i + 1) * 8] for i in range(x.shape[0] // 8)]
    n_acc = min(4, len(parts))
    accs = parts[:n_acc]
    for i, part in enumerate(parts[n_acc:]):
        accs[i % n_acc] = op(accs[i % n_acc], part)
    while len(accs) > 1:
        accs = [op(accs[i], accs[i + 1]) for i in range(0, len(accs), 2)]
    return accs[0]


def _dsa_kernel(qit_ref, kidx_ref, wt_ref, qt_ref, k_ref, vt_ref, nw_ref, o_ref, sc_ref, lo_ref, hi_ref, *, n_sel):
    s_len = k_ref.shape[1]
    n_rb = s_len // QB
    rg = qt_ref.shape[-1] // 2
    n_rg = s_len // rg
    sup = rg
    kf = float(n_sel)
    key_i = lax.broadcasted_iota(jnp.int32, (QB, QB), 0)
    row_i = lax.broadcasted_iota(jnp.int32, (QB, QB), 1)
    key_g = lax.broadcasted_iota(jnp.int32, (QB, rg), 0)
    row_g = lax.broadcasted_iota(jnp.int32, (QB, rg), 1)

    def col_sum(x):
        return jnp.sum(x, axis=0, keepdims=True)

    def score_rows(g, carry):
        goff = pl.multiple_of(g * rg, rg)
        w = wt_ref[0, :, pl.ds(goff, rg)]

        def score_keys(c, carry2):
            for p in range(sup // QB):
                off = pl.multiple_of(c * sup + p * QB, QB)
                lg = _dot(kidx_ref[0, pl.ds(off, QB), :], qit_ref[0, g])
                acc = jnp.zeros((QB, rg), f32)
                for h in range(IDX_HEADS):
                    acc = acc + w[h:h + 1, :] * jnp.maximum(lg[:, h * rg:(h + 1) * rg], 0.0)
                sc_ref[pl.ds(off, QB), pl.ds(goff, rg)] = jnp.where(off + key_g <= goff + row_g, acc, -jnp.inf)
            return carry2

        lax.fori_loop(0, g + 1, score_keys, 0)
        return carry

    lax.fori_loop(0, n_rg, score_rows, 0)

    first = n_sel // QB
    ones_lhs = jnp.ones((16, s_len), bf16)

    def floor_avg(a, b):
        return (a & b) + ((a ^ b) >> 1)

    if (rg // QB) % 2 == 0:
        units = [(first, 1)] if first % 2 else []
        units += [(r, 2) for r in range(first + first % 2, n_rb - 1, 2)]
        if (n_rb - first - first % 2) % 2:
            units.append((n_rb - 1, 1))
    else:
        units = [(r, 1) for r in range(first, n_rb)]

    def search_pass(_, carry):
        for r, width in units:
            blocks = range(r, r + width)
            lo = jnp.concatenate([lo_ref[b, 0:1, :] for b in blocks], axis=1)
            hi = jnp.concatenate([hi_ref[b, 0:1, :] for b in blocks], axis=1)
            mid = floor_avg(lo, hi)
            n_keys = (r + width) * QB
            hit = jnp.where(sc_ref[0:n_keys, r * QB:(r + width) * QB] >= _key_to_f32(mid), 1.0, 0.0).astype(bf16)
            ge = _dot(ones_lhs[:, :n_keys], hit)[0:1] >= kf
            new_lo = jnp.where(ge, mid, lo)
            new_hi = jnp.where(ge, hi, mid)
            for i, b in enumerate(blocks):
                lo_ref[b, 0:1, :] = new_lo[:, i * QB:(i + 1) * QB]
                hi_ref[b, 0:1, :] = new_hi[:, i * QB:(i + 1) * QB]
        return carry

    lo_ref[...] = jnp.full(lo_ref.shape, KEY_NEG_INF, jnp.int32)
    hi_ref[...] = jnp.full(hi_ref.shape, KEY_POS_INF, jnp.int32)
    lax.fori_loop(0, 32, search_pass, 0)

    def finalize(r, carry):
        roff = pl.multiple_of(r * QB, QB)
        thr = _key_to_f32(lo_ref[r, 0:1, :])

        def count(pred):
            def body(c, part):
                off = pl.multiple_of(c * QB, QB)
                blk = sc_ref[pl.ds(off, QB), pl.ds(roff, QB)]
                return part + jnp.where(pred(blk, off + key_i), 1.0, 0.0)
            return col_sum(lax.fori_loop(0, r + 1, body, jnp.zeros((QB, QB), f32)))

        finite = thr > -jnp.inf
        tied = jnp.logical_and(count(lambda blk, pos: blk >= thr) != kf, finite)
        cut0 = jnp.where(finite, s_len, -1)

        def break_ties(_):
            need = kf - count(lambda blk, pos: blk > thr)

            def bisect_idx(_, lh):
                lo_j, hi_j = lh
                mid = (lo_j + hi_j) >> 1
                ok = count(lambda blk, pos: jnp.logical_and(blk == thr, pos <= mid)) >= need
                return jnp.where(ok, lo_j, mid), jnp.where(ok, mid, hi_j)

            n_it = int(np.ceil(np.log2(s_len + 1)))
            _, hi_j = lax.fori_loop(0, n_it, bisect_idx, (jnp.full((1, QB), -1, jnp.int32),
                                                          jnp.full((1, QB), s_len - 1, jnp.int32)))
            return jnp.where(tied, hi_j, cut0)

        cut = lax.cond(jnp.max(tied.astype(jnp.int32)) > 0, break_ties, lambda _: cut0, 0)

        def write_mask(c, carry2):
            off = pl.multiple_of(c * QB, QB)
            blk = sc_ref[pl.ds(off, QB), pl.ds(roff, QB)]
            sel = jnp.logical_or(blk > thr, jnp.logical_and(blk == thr, off + key_i <= cut))
            sc_ref[pl.ds(off, QB), pl.ds(roff, QB)] = jnp.where(sel, 0.0, NEG_BIG)
            return carry2

        lax.fori_loop(0, per_g * (r // per_g + 1), write_mask, 0)
        return carry

    per_g = rg // QB

    any_tied = jnp.zeros((1, QB), jnp.int32)
    for r, width in units:
        n_keys = (r + width) * QB
        thr = jnp.concatenate([_key_to_f32(lo_ref[b, 0:1, :]) for b in range(r, r + width)], axis=1)
        hit = jnp.where(sc_ref[0:n_keys, r * QB:(r + width) * QB] >= thr, 1.0, 0.0).astype(bf16)
        over = (_dot(ones_lhs[:, :n_keys], hit)[0:1] != kf).astype(jnp.int32)
        for i in range(width):
            any_tied = jnp.maximum(any_tied, over[:, i * QB:(i + 1) * QB])

    def with_ties(_):
        lax.fori_loop(0, n_rb, finalize, 0)
        return 0

    def without_ties(_):
        for r in range(n_rb):
            thr = _key_to_f32(lo_ref[r, 0:1, :])
            thr = jnp.maximum(thr, float(jnp.finfo(f32).min))
            for c in range(per_g * (r // per_g + 1)):
                tile = (slice(c * QB, (c + 1) * QB), slice(r * QB, (r + 1) * QB))
                sc_ref[tile] = jnp.where(sc_ref[tile] >= thr, 0.0, NEG_BIG)
        return 0

    lax.cond(jnp.max(any_tied) > 0, with_ties, without_ties, 0)

    def attend(g, carry, sup=sup // ATT_KEY_SPLIT):
        goff = pl.multiple_of(g * rg, rg)
        n_sup = (g + 1) * ATT_KEY_SPLIT

        ones_rows = jnp.ones((16, sup), bf16)
        outs = []
        for h0 in range(0, ATT_HEADS, ATT_HEADS_PER_LOOP):
            heads = range(h0, h0 + ATT_HEADS_PER_LOOP)

            def key_step(c, carry2, heads=heads):
                m_run, acc = carry2
                off = pl.multiple_of(c * sup, sup)
                bias = sc_ref[pl.ds(off, sup), pl.ds(goff, rg)]
                s_all = []
                for hp in range(heads[0] // 2, heads[-1] // 2 + 1):
                    k_pair = k_ref[0, pl.ds(off, sup), hp * 2 * HEAD_DIM:(hp + 1) * 2 * HEAD_DIM]
                    s2 = _dot(k_pair, qt_ref[0, g, hp])
                    s_all += [s2[:, :rg] + bias, s2[:, rg:] + bias]
                m_new, acc_new = [], []
                for i, h in enumerate(heads):
                    m_h = jnp.maximum(m_run[i], jnp.max(_fold8(s_all[i], jnp.maximum), axis=0, keepdims=True))
                    alpha = jnp.exp2(m_run[i] - m_h)
                    p = jnp.exp2(s_all[i] - m_h).astype(bf16)
                    vt_h = jnp.concatenate([vt_ref[0, h * HEAD_DIM:(h + 1) * HEAD_DIM, pl.ds(off, sup)], ones_rows],
                                           axis=0)
                    acc_new.append(alpha * acc[i] + _dot(vt_h, p))
                    m_new.append(m_h)
                return tuple(m_new), tuple(acc_new)

            _, acc = lax.fori_loop(0, n_sup, key_step,
                                   (tuple(jnp.full((1, rg), NEG_BIG, f32) for _ in heads),
                                    tuple(jnp.zeros((HEAD_DIM + 16, rg), f32) for _ in heads)))
            for i, h in enumerate(heads):
                o = acc[i][:HEAD_DIM] / acc[i][HEAD_DIM:HEAD_DIM + 1]
                inv = lax.rsqrt(jnp.mean(o * o, axis=0, keepdims=True) + EPS)
                outs.append(o * inv * nw_ref[h * HEAD_DIM:(h + 1) * HEAD_DIM, :])
        o_ref[0, pl.ds(goff, rg), :] = jnp.concatenate(outs, axis=0).T.astype(bf16)
        return carry

    lax.fori_loop(0, n_rg, attend, 0)


def _dsa(qt, qit, k, vt, kidx, wt, nw):
    b, s, _ = k.shape
    n_rb = s // QB
    n_sel = min(TOPK, s // 4)
    rg = qt.shape[-1] // 2
    s_pad = -(-s // rg) * rg
    kernel = functools.partial(_dsa_kernel, n_sel=n_sel)

    def per_batch(a):
        return pl.BlockSpec((1,) + a.shape[1:], lambda i: (i,) + (0,) * (a.ndim - 1))

    nw_col = nw.reshape(ATT_WIDTH, 1)
    return pl.pallas_call(
        kernel,
        out_shape=jax.ShapeDtypeStruct((b, s, ATT_WIDTH), bf16),
        grid=(b,),
        in_specs=[per_batch(qit), per_batch(kidx), per_batch(wt), per_batch(qt), per_batch(k), per_batch(vt),
                  pl.BlockSpec((ATT_WIDTH, 1), lambda i: (0, 0))],
        out_specs=pl.BlockSpec((1, s, ATT_WIDTH), lambda i: (i, 0, 0)),
        scratch_shapes=[pltpu.VMEM((s_pad, s), f32),
                        pltpu.VMEM((n_rb, 8, QB), jnp.int32),
                        pltpu.VMEM((n_rb, 8, QB), jnp.int32)],
        compiler_params=pltpu.CompilerParams(dimension_semantics=("arbitrary",),
                                             vmem_limit_bytes=VMEM_LIMIT),
        name="dsa_attention",
    )(qit, kidx, wt, qt, k, vt, nw_col)


ML_STATE_ROWS = ML_DIM + 16


def _mlstm_kernel(qm_ref, km_ref, vm_ref, om_ref, gt_ref, nw_ref, hm_ref, c_ref, cum_ref, gcol_ref, *, chunk):
    s_len = qm_ref.shape[1]
    n_chunks = s_len // chunk
    s_i = lax.broadcasted_iota(jnp.int32, (chunk, chunk), 0)
    t_i = lax.broadcasted_iota(jnp.int32, (chunk, chunk), 1)
    causal = s_i <= t_i
    tri_t = jnp.where(causal, 1.0, 0.0).astype(f32)
    first_row = lax.broadcasted_iota(jnp.int32, (ML_STATE_ROWS - ML_DIM, chunk), 0) == 0
    c_ref[...] = jnp.zeros(c_ref.shape, f32)

    n_g = 2 * ML_HEADS
    g_all = jnp.concatenate([gt_ref[0, :, c * chunk:(c + 1) * chunk] for c in range(n_chunks)], axis=0)
    cum_all = _dot(g_all, tri_t, precision=lax.Precision.HIGHEST)
    cum_ref[...] = cum_all
    for c in range(n_chunks):
        g_src = g_all[c * n_g:c * n_g + ML_HEADS] - cum_all[c * n_g + ML_HEADS:(c + 1) * n_g]
        gcol_ref[c] = jnp.concatenate([g_src, jnp.zeros((LANE - ML_HEADS, chunk), f32)], axis=0).T

    def body(c, m_state):
        off = pl.multiple_of(c * chunk, chunk)
        g_row = gt_ref[0, :, pl.ds(off, chunk)]
        cum_row = cum_ref[pl.ds(pl.multiple_of(c * n_g, n_g), n_g), :]
        g_src_col = gcol_ref[c]
        m_next = []
        for h in range(ML_HEADS):
            ig_row = g_row[h:h + 1, :]
            b_row = cum_row[ML_HEADS + h:ML_HEADS + h + 1, :]
            m_s = m_state[h]
            total = b_row[:, chunk - 1:chunk]
            hs = slice(h * ML_DIM, (h + 1) * ML_DIM)
            q_t = qm_ref[0, pl.ds(off, chunk), hs].astype(f32).T.astype(bf16)
            k_h = km_ref[0, pl.ds(off, chunk), hs]
            v_tf = vm_ref[0, pl.ds(off, chunk), hs].astype(f32).T

            d_t = jnp.where(causal, g_src_col[:, h:h + 1] + b_row, -jnp.inf)
            inter = b_row + m_s
            m_row = jnp.maximum(inter, jnp.max(d_t, axis=0, keepdims=True))
            w_inter = jnp.exp(inter - m_row)
            s_t = _dot(k_h, q_t) * jnp.exp(d_t - m_row)
            c_ext = c_ref[h]
            qc = _dot(c_ext.astype(bf16), q_t)
            num = _dot(v_tf.astype(bf16), s_t.astype(bf16)) + w_inter * qc[:ML_DIM]
            den = jnp.sum(s_t, axis=0, keepdims=True) + w_inter * qc[ML_DIM:ML_DIM + 1]
            hh = num / jnp.maximum(jnp.abs(den), jnp.exp(-m_row))

            w_log = total - b_row + ig_row
            m_new = jnp.maximum(total + m_s, jnp.max(w_log, axis=1, keepdims=True))
            decay = jnp.exp(total + m_s - m_new)
            w_state = jnp.exp(w_log - m_new)
            lhs = jnp.concatenate([(v_tf * w_state).astype(bf16),
                                   jnp.where(first_row, w_state, 0.0).astype(bf16)], axis=0)
            c_ref[h] = decay * c_ext + _dot(lhs, k_h)
            m_next.append(m_new)

            inv = lax.rsqrt(jnp.mean(hh * hh, axis=0, keepdims=True) + EPS)
            gate = jax.nn.sigmoid(om_ref[0, pl.ds(off, chunk), hs].T)
            out_t = gate * hh * inv * nw_ref[hs, :]
            hm_ref[0, pl.ds(off, chunk), hs] = out_t.T.astype(bf16)
        return tuple(m_next)

    unroll = next(u for u in (4, 2, 1) if n_chunks % u == 0)
    lax.fori_loop(0, n_chunks, body, tuple(jnp.zeros((1, 1), f32) for _ in range(ML_HEADS)), unroll=unroll)


def _mlstm(qm, km, vm, om, gt, nw):
    b, s, _ = qm.shape
    chunk = min(ML_CHUNK, s)
    kernel = functools.partial(_mlstm_kernel, chunk=chunk)
    tok = pl.BlockSpec((1, s, ML_WIDTH), lambda i: (i, 0, 0))
    return pl.pallas_call(
        kernel,
        out_shape=jax.ShapeDtypeStruct((b, s, ML_WIDTH), bf16),
        grid=(b,),
        in_specs=[tok, tok, tok, tok,
                  pl.BlockSpec((1, 2 * ML_HEADS, s), lambda i: (i, 0, 0)),
                  pl.BlockSpec((ML_WIDTH, 1), lambda i: (0, 0))],
        out_specs=tok,
        scratch_shapes=[pltpu.VMEM((ML_HEADS, ML_STATE_ROWS, ML_DIM), f32),
                        pltpu.VMEM((s // chunk * 2 * ML_HEADS, chunk), f32),
                        pltpu.VMEM((s // chunk, chunk, LANE), f32)],
        compiler_params=pltpu.CompilerParams(dimension_semantics=("arbitrary",),
                                             vmem_limit_bytes=VMEM_LIMIT),
        name="mlstm",
    )(qm, km, vm, om, gt, nw.reshape(ML_WIDTH, 1))


def _out_ffn_kernel(att_ref, hm_ref, x_ref, mod_ref, wout_ref, npost_ref, fpre_ref, wgu_ref, wdn_ref, fpost_ref,
                    o_ref, *, bounds):
    hid = wdn_ref.shape[0]
    g_m = mod_ref[0, 2:3, :]
    sh_f = mod_ref[0, 3:4, :]
    sc_f = mod_ref[0, 4:5, :]
    g_f = mod_ref[0, 5:6, :]
    tm = x_ref.shape[1]
    piece = tm // FFN_ROW_PIECES
    for r0 in range(0, tm, piece):
        rows = slice(r0, r0 + piece)
        y = _dot(att_ref[0, rows, :], wout_ref[0:ATT_WIDTH, :]) + _dot(hm_ref[0, rows, :], wout_ref[ATT_WIDTH:, :])
        x1 = x_ref[0, rows, :] + g_m * _rms(y, npost_ref[...])
        h2 = (_rms(x1, fpre_ref[...]) * (1.0 + sc_f) + sh_f).astype(bf16)
        y2 = jnp.zeros(x1.shape, f32)
        for c0, c1 in zip(bounds[:-1], bounds[1:]):
            g = _dot(h2, wgu_ref[:, c0:c1])
            u = _dot(h2, wgu_ref[:, hid + c0:hid + c1])
            a = (g * jax.nn.sigmoid(g) * u).astype(bf16)
            y2 = y2 + _dot(a, wdn_ref[c0:c1, :])
        o_ref[0, rows, :] = x1 + g_f * _rms(y2, fpost_ref[...])


def _out_ffn(att, hm, x, mod_l, wout, npost, fpre, wgu, wdn, fpost, tm, layer):
    b, s, d = x.shape
    hid = wdn.shape[1]
    mxu_tile = 256
    cut = -(-(hid // 2) // mxu_tile) * mxu_tile
    bounds = (0, cut, hid) if 0 < cut < hid else (0, hid)

    def tok(w):
        return pl.BlockSpec((1, tm, w), lambda i, j: (i, j, 0))

    def const(a):
        return pl.BlockSpec(a.shape, lambda i, j: (0,) * a.ndim, pipeline_mode=pl.Buffered(1))

    def of_layer(a):
        return pl.BlockSpec((None,) + a.shape[1:], lambda i, j: (layer,) + (0,) * (a.ndim - 1),
                            pipeline_mode=pl.Buffered(1))

    kernel = functools.partial(_out_ffn_kernel, bounds=bounds)
    return pl.pallas_call(
        kernel,
        out_shape=jax.ShapeDtypeStruct((b, s, d), f32),
        grid=(b, s // tm),
        in_specs=[tok(ATT_WIDTH), tok(ML_WIDTH), tok(d), pl.BlockSpec((1, 6, d), lambda i, j: (i, 0, 0)),
                  of_layer(wout), const(npost), const(fpre), of_layer(wgu), of_layer(wdn), const(fpost)],
        out_specs=tok(d),
        compiler_params=pltpu.CompilerParams(dimension_semantics=("arbitrary", "arbitrary"),
                                             vmem_limit_bytes=VMEM_LIMIT),
        name="out_ffn",
    )(att, hm, x, mod_l, wout, npost, fpre, wgu, wdn, fpost)


def _arrange_w_in(w):
    pts = [int(p) for p in np.cumsum(IN_SPLITS)[:-1]]
    cq, k_att, v_att, k_idx, w_idx, q_m, k_m, v_m, o_m, i_m, f_m = jnp.split(w, pts, axis=-1)
    misc = jnp.concatenate([k_idx, w_idx, i_m, f_m, jnp.zeros(w.shape[:-1] + (LANE - MISC_END,), w.dtype)], axis=-1)
    return jnp.concatenate([cq, k_att, v_att, misc, q_m, k_m, v_m, o_m], axis=-1).astype(bf16)


def kernel(x, c, positions, w_mod, b_mod, mix_norm_pre, mix_norm_post, w_in, q_latent_norm, w_q_up, w_qidx_up,
           conv_w, conv_b, b_igate, b_fgate, attn_out_norm, mlstm_out_norm, w_out, ffn_norm_pre, ffn_norm_post,
           w_gate_up, w_down):
    depth = w_mod.shape[0]
    b, s, d = x.shape
    tm = min(512, s)
    mod = _modulation(c, w_mod, b_mod).reshape(depth, b, 6, d)
    cos, sin = _rope_tables(positions)

    def row(a):
        return a.reshape(1, -1)

    win = _arrange_w_in(w_in)
    wq = jnp.concatenate([w_q_up, w_qidx_up], axis=-1).astype(bf16)
    wout, wgu, wdn = w_out.astype(bf16), w_gate_up.astype(bf16), w_down.astype(bf16)
    for l in range(depth):
        gb = jnp.zeros((1, LANE), f32).at[0, MISC_I:MISC_F].set(b_igate[l]).at[0, MISC_F:MISC_END].set(b_fgate[l])
        qt, qit, k, vt, kidx, wt, gt, qm, km, vm, om = _in_proj(
            x, mod[l], row(mix_norm_pre[l]), win, row(q_latent_norm[l]), wq, conv_w[l], row(conv_b[l]), gb,
            cos, sin, tm, l)
        att = _dsa(qt, qit, k, vt, kidx, wt, attn_out_norm[l])
        hm = _mlstm(qm, km, vm, om, gt, mlstm_out_norm[l])
        x = _out_ffn(att, hm, x, mod[l], wout, row(mix_norm_post[l]), row(ffn_norm_pre[l]), wgu, wdn,
                     row(ffn_norm_post[l]), tm, l)
    return x
```

```python
import functools

import numpy as np
import jax
import jax.numpy as jnp
from jax import lax
from jax.experimental import pallas as pl
from jax.experimental.pallas import tpu as pltpu

EPS = 1e-6
ROPE_THETA = 10000.0
ATT_HEADS = 8
HEAD_DIM = 64
ATT_WIDTH = ATT_HEADS * HEAD_DIM
Q_RANK = 256
IDX_HEADS = 8
ML_HEADS = 4
ML_DIM = 128
ML_WIDTH = ML_HEADS * ML_DIM
CONV_W = 4
TOPK = 256
QB = 128
ATT_HEADS_PER_LOOP = 8
ATT_KEY_SPLIT = 1
FFN_ROW_PIECES = 1
LANE = 128
ML_CHUNK = 256
IN_SPLITS = (Q_RANK, ATT_WIDTH, ATT_WIDTH, HEAD_DIM, IDX_HEADS,
             ML_WIDTH, ML_WIDTH, ML_WIDTH, ML_WIDTH, ML_HEADS, ML_HEADS)

OFF_CQ = 0
OFF_K = OFF_CQ + Q_RANK
OFF_V = OFF_K + ATT_WIDTH
OFF_MISC = OFF_V + ATT_WIDTH
OFF_QKM = OFF_MISC + LANE
OFF_VM = OFF_QKM + 2 * ML_WIDTH
OFF_OM = OFF_VM + ML_WIDTH
IN_COLS_PADDED = OFF_OM + ML_WIDTH
MISC_W = HEAD_DIM
MISC_I = MISC_W + IDX_HEADS
MISC_F = MISC_I + ML_HEADS
MISC_END = MISC_F + ML_HEADS

VMEM_LIMIT = 56 * 1024 * 1024
NEG_BIG = -1e30
LOG2E = float(np.log2(np.e))
KEY_POS_INF = 0x7F800000
KEY_NEG_INF = (0xFF800000 - (1 << 32)) ^ 0x7FFFFFFF

bf16 = jnp.bfloat16
f32 = jnp.float32


def _rms(x, w):
    return x * lax.rsqrt(jnp.mean(x * x, axis=-1, keepdims=True) + EPS) * w


def _dot(a, b, precision=None):
    return jnp.dot(a, b, preferred_element_type=f32, precision=precision)


def _rope(x, cos, sin_signed):
    n = x.shape[1]
    lane = lax.broadcasted_iota(jnp.int32, x.shape, 1)
    low = (lane & (HEAD_DIM // 2)) == 0
    rot = jnp.where(low, pltpu.roll(x, n - HEAD_DIM // 2, 1), pltpu.roll(x, HEAD_DIM // 2, 1))
    return x * cos + rot * sin_signed


def _mod_kernel(c_ref, w_ref, b_ref, o_ref):
    c = c_ref[...]
    c_act = c * jax.nn.sigmoid(c)
    o_ref[0] = _dot(c_act, w_ref[0], precision=lax.Precision.HIGHEST) + b_ref[0]


def _modulation(c, w_mod, b_mod):
    depth, d, d6 = w_mod.shape
    b = c.shape[0]
    nj = d6 // d
    return pl.pallas_call(
        _mod_kernel,
        out_shape=jax.ShapeDtypeStruct((depth, b, d6), f32),
        grid=(depth, nj),
        in_specs=[pl.BlockSpec((b, d), lambda l, j: (0, 0)),
                  pl.BlockSpec((1, d, d), lambda l, j: (l, 0, j)),
                  pl.BlockSpec((1, 1, d), lambda l, j: (l, 0, j))],
        out_specs=pl.BlockSpec((1, b, d), lambda l, j: (l, 0, j)),
        name="modulation",
    )(c, w_mod, b_mod.reshape(depth, 1, d6))


def _rope_table_kernel(pos_ref, invf_ref, sign_ref, cos_ref, sin_ref):
    ang = invf_ref[...] * pos_ref[0].astype(f32)
    reps = LANE // (HEAD_DIM // 2)
    cos_ref[0] = jnp.concatenate([jnp.cos(ang)] * reps, axis=0).T
    sin_ref[0] = jnp.concatenate([jnp.sin(ang)] * reps, axis=0).T * sign_ref[...]


def _rope_tables(positions):
    b, s = positions.shape
    ts = min(s, 512)
    inv_freq = ROPE_THETA ** (-jnp.arange(0, HEAD_DIM, 2, dtype=f32) / HEAD_DIM)
    sign = np.where((np.arange(LANE) % HEAD_DIM) < HEAD_DIM // 2, -1.0, 1.0).astype(np.float32).reshape(1, LANE)
    tab = jax.ShapeDtypeStruct((b, s, LANE), f32)
    return pl.pallas_call(
        _rope_table_kernel,
        out_shape=(tab, tab),
        grid=(b, s // ts),
        in_specs=[pl.BlockSpec((1, 1, ts), lambda i, j: (i, 0, j)),
                  pl.BlockSpec((HEAD_DIM // 2, 1), lambda i, j: (0, 0)),
                  pl.BlockSpec((1, LANE), lambda i, j: (0, 0))],
        out_specs=(pl.BlockSpec((1, ts, LANE), lambda i, j: (i, j, 0)),
                   pl.BlockSpec((1, ts, LANE), lambda i, j: (i, j, 0))),
        name="rope_tables",
    )(positions.reshape(b, 1, s), inv_freq.reshape(HEAD_DIM // 2, 1), jnp.asarray(sign))


def _in_proj_kernel(x_ref, mod_ref, nw_ref, win_ref, qln_ref, wq_ref, cw_ref, cb_ref, gb_ref, cos_ref, sin_ref,
                    qt_ref, qit_ref, k_ref, vt_ref, kidx_ref, wt_ref, gt_ref, qm_ref, km_ref, vm_ref, om_ref,
                    buf_ref):
    tm = x_ref.shape[1]

    @pl.when(pl.program_id(1) == 0)
    def _():
        buf_ref[0:8, :] = jnp.zeros((8, 2 * ML_WIDTH), f32)

    x = x_ref[0]
    sh = mod_ref[0, 0:1, :]
    sc = mod_ref[0, 1:2, :]
    hb = (_rms(x, nw_ref[...]) * (1.0 + sc) + sh).astype(bf16)

    cos = cos_ref[0]
    sin = sin_ref[0]
    reps = ATT_WIDTH // LANE
    cos_w = jnp.concatenate([cos] * reps, axis=1)
    sin_w = jnp.concatenate([sin] * reps, axis=1)

    cq = _dot(hb, win_ref[:, OFF_CQ:OFF_CQ + Q_RANK])
    cqn = _rms(cq, qln_ref[...]).astype(bf16)
    qq = _dot(cqn, wq_ref[...])
    q_t = (_rope(qq[:, :ATT_WIDTH], cos_w, sin_w) * (HEAD_DIM ** -0.5 * LOG2E)).T.astype(bf16)
    qi_t = _rope(qq[:, ATT_WIDTH:], cos_w, sin_w).T.astype(bf16)
    qt_ref[...] = jnp.zeros(qt_ref.shape, bf16)
    for h in range(ATT_HEADS):
        d0 = (h % 2) * HEAD_DIM
        qt_ref[0, 0, h // 2, d0:d0 + HEAD_DIM, (h % 2) * tm:(h % 2 + 1) * tm] = q_t[h * HEAD_DIM:(h + 1) * HEAD_DIM, :]
    for h in range(IDX_HEADS):
        qit_ref[0, 0, :, h * tm:(h + 1) * tm] = qi_t[h * HEAD_DIM:(h + 1) * HEAD_DIM, :]
    k_ref[0] = _rope(_dot(hb, win_ref[:, OFF_K:OFF_K + ATT_WIDTH]), cos_w, sin_w).astype(bf16)
    vt_ref[0] = _dot(hb, win_ref[:, OFF_V:OFF_V + ATT_WIDTH]).T.astype(bf16)

    m = _dot(hb, win_ref[:, OFF_MISC:OFF_MISC + LANE])
    lane = lax.broadcasted_iota(jnp.int32, m.shape, 1)
    roped = _rope(m, cos, sin)
    biased = m + gb_ref[...]
    logsig = jnp.minimum(biased, 0.0) - jnp.log1p(jnp.exp(-jnp.abs(biased)))
    w_scale = (IDX_HEADS * HEAD_DIM) ** -0.5
    misc = jnp.where(lane < MISC_W, roped,
                     jnp.where(lane < MISC_I, m * w_scale,
                               jnp.where(lane < MISC_F, biased,
                                         jnp.where(lane < MISC_END, logsig, 0.0))))
    misc_t = misc.T
    kidx_ref[0] = misc[:, 0:HEAD_DIM].astype(bf16)
    wt_ref[0] = misc_t[MISC_W:MISC_I, :]
    gt_ref[0] = misc_t[MISC_I:MISC_END, :]

    pre = _dot(hb, win_ref[:, OFF_QKM:OFF_QKM + 2 * ML_WIDTH])
    buf_ref[8:8 + tm, :] = pre
    acc = jnp.broadcast_to(cb_ref[...], pre.shape)
    hist = buf_ref[...]
    for t in range(CONV_W):
        shift = CONV_W - 1 - t
        shifted = pltpu.roll(hist, shift, 0) if shift else hist
        acc = acc + cw_ref[t:t + 1, :] * shifted[8:8 + tm, :]
    buf_ref[0:8, :] = buf_ref[tm:tm + 8, :]
    qk = acc * jax.nn.sigmoid(acc)
    qm_ref[0] = (qk[:, :ML_WIDTH] * (ML_DIM ** -0.5)).astype(bf16)
    km_ref[0] = qk[:, ML_WIDTH:].astype(bf16)
    vm_ref[0] = _dot(hb, win_ref[:, OFF_VM:OFF_VM + ML_WIDTH]).astype(bf16)
    om_ref[0] = _dot(hb, win_ref[:, OFF_OM:OFF_OM + ML_WIDTH])


def _in_proj(x, mod_l, nw, win, qln, wq, cw, cb, gb, cos, sin, tm, layer):
    b, s, d = x.shape
    nt = s // tm

    def tok(w):
        return pl.BlockSpec((1, tm, w), lambda i, j: (i, j, 0))

    def tok_t(r):
        return pl.BlockSpec((1, r, tm), lambda i, j: (i, 0, j))

    def full(a):
        return pl.BlockSpec(a.shape, lambda i, j: (0,) * a.ndim)

    def of_layer(a):
        return pl.BlockSpec((None,) + a.shape[1:], lambda i, j: (layer,) + (0,) * (a.ndim - 1))

    sd = jax.ShapeDtypeStruct
    qt_shape = (ATT_HEADS // 2, 2 * HEAD_DIM, 2 * tm)
    qit_shape = (HEAD_DIM, IDX_HEADS * tm)
    out_shape = (sd((b, nt) + qt_shape, bf16), sd((b, nt) + qit_shape, bf16), sd((b, s, ATT_WIDTH), bf16),
                 sd((b, ATT_WIDTH, s), bf16), sd((b, s, HEAD_DIM), bf16), sd((b, IDX_HEADS, s), f32),
                 sd((b, 2 * ML_HEADS, s), f32), sd((b, s, ML_WIDTH), bf16),
                 sd((b, s, ML_WIDTH), bf16), sd((b, s, ML_WIDTH), bf16), sd((b, s, ML_WIDTH), f32))
    out_specs = (pl.BlockSpec((1, 1) + qt_shape, lambda i, j: (i, j, 0, 0, 0)),
                 pl.BlockSpec((1, 1) + qit_shape, lambda i, j: (i, j, 0, 0)),
                 tok(ATT_WIDTH), tok_t(ATT_WIDTH), tok(HEAD_DIM), tok_t(IDX_HEADS),
                 tok_t(2 * ML_HEADS), tok(ML_WIDTH), tok(ML_WIDTH), tok(ML_WIDTH), tok(ML_WIDTH))
    return pl.pallas_call(
        _in_proj_kernel,
        out_shape=out_shape,
        grid=(b, nt),
        in_specs=[tok(d), pl.BlockSpec((1, 6, d), lambda i, j: (i, 0, 0)), full(nw), of_layer(win), full(qln),
                  of_layer(wq),
                  full(cw), full(cb), full(gb), tok(LANE), tok(LANE)],
        out_specs=out_specs,
        scratch_shapes=[pltpu.VMEM((tm + 8, 2 * ML_WIDTH), f32)],
        compiler_params=pltpu.CompilerParams(dimension_semantics=("arbitrary", "arbitrary"),
                                             vmem_limit_bytes=VMEM_LIMIT),
        name="in_proj",
    )(x, mod_l, nw, win, qln, wq, cw, cb, gb, cos, sin)


def _key_to_f32(key):
    bits = jnp.where(key >= 0, key, key ^ jnp.int32(0x7FFFFFFF))
    return lax.bitcast_convert_type(bits, f32)


def _fold8(x, op):
    parts = [x[i * 8:(i + 1) * 8] for i in range(x.shape[0] // 8)]
    n_acc = min(4, len(parts))
    accs = parts[:n_acc]
    for i, part in enumerate(parts[n_acc:]):
        accs[i % n_acc] = op(accs[i % n_acc], part)
    while len(accs) > 1:
        accs = [op(accs[i], accs[i + 1]) for i in range(0, len(accs), 2)]
    return accs[0]


def _dsa_kernel(qit_ref, kidx_ref, wt_ref, qt_ref, k_ref, vt_ref, nw_ref, o_ref, sc_ref, lo_ref, hi_ref, *, n_sel):
    s_len = k_ref.shape[1]
    n_rb = s_len // QB
    rg = qt_ref.shape[-1] // 2
    n_rg = s_len // rg
    sup = rg
    kf = float(n_sel)
    key_i = lax.broadcasted_iota(jnp.int32, (QB, QB), 0)
    row_i = lax.broadcasted_iota(jnp.int32, (QB, QB), 1)
    key_g = lax.broadcasted_iota(jnp.int32, (QB, rg), 0)
    row_g = lax.broadcasted_iota(jnp.int32, (QB, rg), 1)

    def col_sum(x):
        return jnp.sum(x, axis=0, keepdims=True)

    def score_rows(g, carry):
        goff = pl.multiple_of(g * rg, rg)
        w = wt_ref[0, :, pl.ds(goff, rg)]

        def score_keys(c, carry2):
            for p in range(sup // QB):
                off = pl.multiple_of(c * sup + p * QB, QB)
                lg = _dot(kidx_ref[0, pl.ds(off, QB), :], qit_ref[0, g])
                acc = jnp.zeros((QB, rg), f32)
                for h in range(IDX_HEADS):
                    acc = acc + w[h:h + 1, :] * jnp.maximum(lg[:, h * rg:(h + 1) * rg], 0.0)
                sc_ref[pl.ds(off, QB), pl.ds(goff, rg)] = jnp.where(off + key_g <= goff + row_g, acc, -jnp.inf)
            return carry2

        lax.fori_loop(0, g + 1, score_keys, 0)
        return carry

    lax.fori_loop(0, n_rg, score_rows, 0)

    first = n_sel // QB
    ones_lhs = jnp.ones((16, s_len), bf16)

    def floor_avg(a, b):
        return (a & b) + ((a ^ b) >> 1)

    if (rg // QB) % 2 == 0:
        units = [(first, 1)] if first % 2 else []
        units += [(r, 2) for r in range(first + first % 2, n_rb - 1, 2)]
        if (n_rb - first - first % 2) % 2:
            units.append((n_rb - 1, 1))
    else:
        units = [(r, 1) for r in range(first, n_rb)]

    def search_pass(_, carry):
        for r, width in units:
            blocks = range(r, r + width)
            lo = jnp.concatenate([lo_ref[b, 0:1, :] for b in blocks], axis=1)
            hi = jnp.concatenate([hi_ref[b, 0:1, :] for b in blocks], axis=1)
            mid = floor_avg(lo, hi)
            n_keys = (r + width) * QB
            hit = jnp.where(sc_ref[0:n_keys, r * QB:(r + width) * QB] >= _key_to_f32(mid), 1.0, 0.0).astype(bf16)
            ge = _dot(ones_lhs[:, :n_keys], hit)[0:1] >= kf
            new_lo = jnp.where(ge, mid, lo)
            new_hi = jnp.where(ge, hi, mid)
            for i, b in enumerate(blocks):
                lo_ref[b, 0:1, :] = new_lo[:, i * QB:(i + 1) * QB]
                hi_ref[b, 0:1, :] = new_hi[:, i * QB:(i + 1) * QB]
        return carry

    lo_ref[...] = jnp.full(lo_ref.shape, KEY_NEG_INF, jnp.int32)
    hi_ref[...] = jnp.full(hi_ref.shape, KEY_POS_INF, jnp.int32)
    lax.fori_loop(0, 32, search_pass, 0)

    def finalize(r, carry):
        roff = pl.multiple_of(r * QB, QB)
        thr = _key_to_f32(lo_ref[r, 0:1, :])

        def count(pred):
            def body(c, part):
                off = pl.multiple_of(c * QB, QB)
                blk = sc_ref[pl.ds(off, QB), pl.ds(roff, QB)]
                return part + jnp.where(pred(blk, off + key_i), 1.0, 0.0)
            return col_sum(lax.fori_loop(0, r + 1, body, jnp.zeros((QB, QB), f32)))

        finite = thr > -jnp.inf
        tied = jnp.logical_and(count(lambda blk, pos: blk >= thr) != kf, finite)
        cut0 = jnp.where(finite, s_len, -1)

        def break_ties(_):
            need = kf - count(lambda blk, pos: blk > thr)

            def bisect_idx(_, lh):
                lo_j, hi_j = lh
                mid = (lo_j + hi_j) >> 1
                ok = count(lambda blk, pos: jnp.logical_and(blk == thr, pos <= mid)) >= need
                return jnp.where(ok, lo_j, mid), jnp.where(ok, mid, hi_j)

            n_it = int(np.ceil(np.log2(s_len + 1)))
            _, hi_j = lax.fori_loop(0, n_it, bisect_idx, (jnp.full((1, QB), -1, jnp.int32),
                                                          jnp.full((1, QB), s_len - 1, jnp.int32)))
            return jnp.where(tied, hi_j, cut0)

        cut = lax.cond(jnp.max(tied.astype(jnp.int32)) > 0, break_ties, lambda _: cut0, 0)

        def write_mask(c, carry2):
            off = pl.multiple_of(c * QB, QB)
            blk = sc_ref[pl.ds(off, QB), pl.ds(roff, QB)]
            sel = jnp.logical_or(blk > thr, jnp.logical_and(blk == thr, off + key_i <= cut))
            sc_ref[pl.ds(off, QB), pl.ds(roff, QB)] = jnp.where(sel, 0.0, NEG_BIG)
            return carry2

        lax.fori_loop(0, per_g * (r // per_g + 1), write_mask, 0)
        return carry

    per_g = rg // QB

    any_tied = jnp.zeros((1, QB), jnp.int32)
    for r, width in units:
        n_keys = (r + width) * QB
        thr = jnp.concatenate([_key_to_f32(lo_ref[b, 0:1, :]) for b in range(r, r + width)], axis=1)
        hit = jnp.where(sc_ref[0:n_keys, r * QB:(r + width) * QB] >= thr, 1.0, 0.0).astype(bf16)
        over = (_dot(ones_lhs[:, :n_keys], hit)[0:1] != kf).astype(jnp.int32)
        for i in range(width):
            any_tied = jnp.maximum(any_tied, over[:, i * QB:(i + 1) * QB])

    def with_ties(_):
        lax.fori_loop(0, n_rb, finalize, 0)
        return 0

    def without_ties(_):
        for r in range(n_rb):
            thr = _key_to_f32(lo_ref[r, 0:1, :])
            thr = jnp.maximum(thr, float(jnp.finfo(f32).min))
            for c in range(per_g * (r // per_g + 1)):
                tile = (slice(c * QB, (c + 1) * QB), slice(r * QB, (r + 1) * QB))
                sc_ref[tile] = jnp.where(sc_ref[tile] >= thr, 0.0, NEG_BIG)
        return 0

    lax.cond(jnp.max(any_tied) > 0, with_ties, without_ties, 0)

    def attend(g, carry, sup=sup // ATT_KEY_SPLIT):
        goff = pl.multiple_of(g * rg, rg)
        n_sup = (g + 1) * ATT_KEY_SPLIT

        ones_rows = jnp.ones((16, sup), bf16)
        outs = []
        for h0 in range(0, ATT_HEADS, ATT_HEADS_PER_LOOP):
            heads = range(h0, h0 + ATT_HEADS_PER_LOOP)

            def key_step(c, carry2, heads=heads):
                m_run, acc = carry2
                off = pl.multiple_of(c * sup, sup)
                bias = sc_ref[pl.ds(off, sup), pl.ds(goff, rg)]
                s_all = []
                for hp in range(heads[0] // 2, heads[-1] // 2 + 1):
                    k_pair = k_ref[0, pl.ds(off, sup), hp * 2 * HEAD_DIM:(hp + 1) * 2 * HEAD_DIM]
                    s2 = _dot(k_pair, qt_ref[0, g, hp])
                    s_all += [s2[:, :rg] + bias, s2[:, rg:] + bias]
                m_new, acc_new = [], []
                for i, h in enumerate(heads):
                    m_h = jnp.maximum(m_run[i], jnp.max(_fold8(s_all[i], jnp.maximum), axis=0, keepdims=True))
                    alpha = jnp.exp2(m_run[i] - m_h)
                    p = jnp.exp2(s_all[i] - m_h).astype(bf16)
                    vt_h = jnp.concatenate([vt_ref[0, h * HEAD_DIM:(h + 1) * HEAD_DIM, pl.ds(off, sup)], ones_rows],
                                           axis=0)
                    acc_new.append(alpha * acc[i] + _dot(vt_h, p))
                    m_new.append(m_h)
                return tuple(m_new), tuple(acc_new)

            _, acc = lax.fori_loop(0, n_sup, key_step,
                                   (tuple(jnp.full((1, rg), NEG_BIG, f32) for _ in heads),
                                    tuple(jnp.zeros((HEAD_DIM + 16, rg), f32) for _ in heads)))
            for i, h in enumerate(heads):
                o = acc[i][:HEAD_DIM] / acc[i][HEAD_DIM:HEAD_DIM + 1]
                inv = lax.rsqrt(jnp.mean(o * o, axis=0, keepdims=True) + EPS)
                outs.append(o * inv * nw_ref[h * HEAD_DIM:(h + 1) * HEAD_DIM, :])
        o_ref[0, pl.ds(goff, rg), :] = jnp.concatenate(outs, axis=0).T.astype(bf16)
        return carry

    lax.fori_loop(0, n_rg, attend, 0)


def _dsa(qt, qit, k, vt, kidx, wt, nw):
    b, s, _ = k.shape
    n_rb = s // QB
    n_sel = min(TOPK, s // 4)
    rg = qt.shape[-1] // 2
    s_pad = -(-s // rg) * rg
    kernel = functools.partial(_dsa_kernel, n_sel=n_sel)

    def per_batch(a):
        return pl.BlockSpec((1,) + a.shape[1:], lambda i: (i,) + (0,) * (a.ndim - 1))

    nw_col = nw.reshape(ATT_WIDTH, 1)
    return pl.pallas_call(
        kernel,
        out_shape=jax.ShapeDtypeStruct((b, s, ATT_WIDTH), bf16),
        grid=(b,),
        in_specs=[per_batch(qit), per_batch(kidx), per_batch(wt), per_batch(qt), per_batch(k), per_batch(vt),
                  pl.BlockSpec((ATT_WIDTH, 1), lambda i: (0, 0))],
        out_specs=pl.BlockSpec((1, s, ATT_WIDTH), lambda i: (i, 0, 0)),
        scratch_shapes=[pltpu.VMEM((s_pad, s), f32),
                        pltpu.VMEM((n_rb, 8, QB), jnp.int32),
                        pltpu.VMEM((n_rb, 8, QB), jnp.int32)],
        compiler_params=pltpu.CompilerParams(dimension_semantics=("arbitrary",),
                                             vmem_limit_bytes=VMEM_LIMIT),
        name="dsa_attention",
    )(qit, kidx, wt, qt, k, vt, nw_col)


ML_STATE_ROWS = ML_DIM + 16


def _mlstm_kernel(qm_ref, km_ref, vm_ref, om_ref, gt_ref, nw_ref, hm_ref, c_ref, cum_ref, gcol_ref, *, chunk):
    s_len = qm_ref.shape[1]
    n_chunks = s_len // chunk
    s_i = lax.broadcasted_iota(jnp.int32, (chunk, chunk), 0)
    t_i = lax.broadcasted_iota(jnp.int32, (chunk, chunk), 1)
    causal = s_i <= t_i
    tri_t = jnp.where(causal, 1.0, 0.0).astype(f32)
    first_row = lax.broadcasted_iota(jnp.int32, (ML_STATE_ROWS - ML_DIM, chunk), 0) == 0
    c_ref[...] = jnp.zeros(c_ref.shape, f32)

    n_g = 2 * ML_HEADS
    g_all = jnp.concatenate([gt_ref[0, :, c * chunk:(c + 1) * chunk] for c in range(n_chunks)], axis=0)
    cum_all = _dot(g_all, tri_t, precision=lax.Precision.HIGHEST)
    cum_ref[...] = cum_all
    for c in range(n_chunks):
        g_src = g_all[c * n_g:c * n_g + ML_HEADS] - cum_all[c * n_g + ML_HEADS:(c + 1) * n_g]
        gcol_ref[c] = jnp.concatenate([g_src, jnp.zeros((LANE - ML_HEADS, chunk), f32)], axis=0).T

    def body(c, m_state):
        off = pl.multiple_of(c * chunk, chunk)
        g_row = gt_ref[0, :, pl.ds(off, chunk)]
        cum_row = cum_ref[pl.ds(pl.multiple_of(c * n_g, n_g), n_g), :]
        g_src_col = gcol_ref[c]
        m_next = []
        for h in range(ML_HEADS):
            ig_row = g_row[h:h + 1, :]
            b_row = cum_row[ML_HEADS + h:ML_HEADS + h + 1, :]
            m_s = m_state[h]
            total = b_row[:, chunk - 1:chunk]
            hs = slice(h * ML_DIM, (h + 1) * ML_DIM)
            q_t = qm_ref[0, pl.ds(off, chunk), hs].astype(f32).T.astype(bf16)
            k_h = km_ref[0, pl.ds(off, chunk), hs]
            v_tf = vm_ref[0, pl.ds(off, chunk), hs].astype(f32).T

            d_t = jnp.where(causal, g_src_col[:, h:h + 1] + b_row, -jnp.inf)
            inter = b_row + m_s
            m_row = jnp.maximum(inter, jnp.max(d_t, axis=0, keepdims=True))
            w_inter = jnp.exp(inter - m_row)
            s_t = _dot(k_h, q_t) * jnp.exp(d_t - m_row)
            c_ext = c_ref[h]
            qc = _dot(c_ext.astype(bf16), q_t)
            num = _dot(v_tf.astype(bf16), s_t.astype(bf16)) + w_inter * qc[:ML_DIM]
            den = jnp.sum(s_t, axis=0, keepdims=True) + w_inter * qc[ML_DIM:ML_DIM + 1]
            hh = num / jnp.maximum(jnp.abs(den), jnp.exp(-m_row))

            w_log = total - b_row + ig_row
            m_new = jnp.maximum(total + m_s, jnp.max(w_log, axis=1, keepdims=True))
            decay = jnp.exp(total + m_s - m_new)
            w_state = jnp.exp(w_log - m_new)
            lhs = jnp.concatenate([(v_tf * w_state).astype(bf16),
                                   jnp.where(first_row, w_state, 0.0).astype(bf16)], axis=0)
            c_ref[h] = decay * c_ext + _dot(lhs, k_h)
            m_next.append(m_new)

            inv = lax.rsqrt(jnp.mean(hh * hh, axis=0, keepdims=True) + EPS)
            gate = jax.nn.sigmoid(om_ref[0, pl.ds(off, chunk), hs].T)
            out_t = gate * hh * inv * nw_ref[hs, :]
            hm_ref[0, pl.ds(off, chunk), hs] = out_t.T.astype(bf16)
        return tuple(m_next)

    unroll = next(u for u in (4, 2, 1) if n_chunks % u == 0)
    lax.fori_loop(0, n_chunks, body, tuple(jnp.zeros((1, 1), f32) for _ in range(ML_HEADS)), unroll=unroll)


def _mlstm(qm, km, vm, om, gt, nw):
    b, s, _ = qm.shape
    chunk = min(ML_CHUNK, s)
    kernel = functools.partial(_mlstm_kernel, chunk=chunk)
    tok = pl.BlockSpec((1, s, ML_WIDTH), lambda i: (i, 0, 0))
    return pl.pallas_call(
        kernel,
        out_shape=jax.ShapeDtypeStruct((b, s, ML_WIDTH), bf16),
        grid=(b,),
        in_specs=[tok, tok, tok, tok,
                  pl.BlockSpec((1, 2 * ML_HEADS, s), lambda i: (i, 0, 0)),
                  pl.BlockSpec((ML_WIDTH, 1), lambda i: (0, 0))],
        out_specs=tok,
        scratch_shapes=[pltpu.VMEM((ML_HEADS, ML_STATE_ROWS, ML_DIM), f32),
                        pltpu.VMEM((s // chunk * 2 * ML_HEADS, chunk), f32),
                        pltpu.VMEM((s // chunk, chunk, LANE), f32)],
        compiler_params=pltpu.CompilerParams(dimension_semantics=("arbitrary",),
                                             vmem_limit_bytes=VMEM_LIMIT),
        name="mlstm",
    )(qm, km, vm, om, gt, nw.reshape(ML_WIDTH, 1))


def _out_ffn_kernel(att_ref, hm_ref, x_ref, mod_ref, wout_ref, npost_ref, fpre_ref, wgu_ref, wdn_ref, fpost_ref,
                    o_ref, *, bounds):
    hid = wdn_ref.shape[0]
    g_m = mod_ref[0, 2:3, :]
    sh_f = mod_ref[0, 3:4, :]
    sc_f = mod_ref[0, 4:5, :]
    g_f = mod_ref[0, 5:6, :]
    tm = x_ref.shape[1]
    piece = tm // FFN_ROW_PIECES
    for r0 in range(0, tm, piece):
        rows = slice(r0, r0 + piece)
        y = _dot(att_ref[0, rows, :], wout_ref[0:ATT_WIDTH, :]) + _dot(hm_ref[0, rows, :], wout_ref[ATT_WIDTH:, :])
        x1 = x_ref[0, rows, :] + g_m * _rms(y, npost_ref[...])
        h2 = (_rms(x1, fpre_ref[...]) * (1.0 + sc_f) + sh_f).astype(bf16)
        y2 = None
        for c0, c1 in zip(bounds[:-1], bounds[1:]):
            g = _dot(h2, wgu_ref[:, c0:c1])
            u = _dot(h2, wgu_ref[:, hid + c0:hid + c1])
            a = (g * jax.nn.sigmoid(g) * u).astype(bf16)
            part = _dot(a, wdn_ref[c0:c1, :])
            y2 = part if y2 is None else y2 + part
        o_ref[0, rows, :] = x1 + g_f * _rms(y2, fpost_ref[...])


def _out_ffn(att, hm, x, mod_l, wout, npost, fpre, wgu, wdn, fpost, tm, layer):
    b, s, d = x.shape
    hid = wdn.shape[1]
    mxu_tile = 256
    cut = -(-(hid // 2) // mxu_tile) * mxu_tile
    bounds = (0, cut, hid) if 0 < cut < hid else (0, hid)

    def tok(w):
        return pl.BlockSpec((1, tm, w), lambda i, j: (i, j, 0))

    def const(a):
        return pl.BlockSpec(a.shape, lambda i, j: (0,) * a.ndim, pipeline_mode=pl.Buffered(1))

    def of_layer(a):
        return pl.BlockSpec((None,) + a.shape[1:], lambda i, j: (layer,) + (0,) * (a.ndim - 1),
                            pipeline_mode=pl.Buffered(1))

    kernel = functools.partial(_out_ffn_kernel, bounds=bounds)
    return pl.pallas_call(
        kernel,
        out_shape=jax.ShapeDtypeStruct((b, s, d), f32),
        grid=(b, s // tm),
        in_specs=[tok(ATT_WIDTH), tok(ML_WIDTH), tok(d), pl.BlockSpec((1, 6, d), lambda i, j: (i, 0, 0)),
                  of_layer(wout), const(npost), const(fpre), of_layer(wgu), of_layer(wdn), const(fpost)],
        out_specs=tok(d),
        compiler_params=pltpu.CompilerParams(dimension_semantics=("arbitrary", "arbitrary"),
                                             vmem_limit_bytes=VMEM_LIMIT),
        name="out_ffn",
    )(att, hm, x, mod_l, wout, npost, fpre, wgu, wdn, fpost)


def _arrange_w_in(w):
    pts = [int(p) for p in np.cumsum(IN_SPLITS)[:-1]]
    cq, k_att, v_att, k_idx, w_idx, q_m, k_m, v_m, o_m, i_m, f_m = jnp.split(w, pts, axis=-1)
    misc = jnp.concatenate([k_idx, w_idx, i_m, f_m, jnp.zeros(w.shape[:-1] + (LANE - MISC_END,), w.dtype)], axis=-1)
    return jnp.concatenate([cq, k_att, v_att, misc, q_m, k_m, v_m, o_m], axis=-1).astype(bf16)


def kernel(x, c, positions, w_mod, b_mod, mix_norm_pre, mix_norm_post, w_in, q_latent_norm, w_q_up, w_qidx_up,
           conv_w, conv_b, b_igate, b_fgate, attn_out_norm, mlstm_out_norm, w_out, ffn_norm_pre, ffn_norm_post,
           w_gate_up, w_down):
    depth = w_mod.shape[0]
    b, s, d = x.shape
    tm = min(512, s)
    mod = _modulation(c, w_mod, b_mod).reshape(depth, b, 6, d)
    cos, sin = _rope_tables(positions)

    def row(a):
        return a.reshape(1, -1)

    win = _arrange_w_in(w_in)
    wq = jnp.concatenate([w_q_up, w_qidx_up], axis=-1).astype(bf16)
    wout, wgu, wdn = w_out.astype(bf16), w_gate_up.astype(bf16), w_down.astype(bf16)
    for l in range(depth):
        gb = jnp.zeros((1, LANE), f32).at[0, MISC_I:MISC_F].set(b_igate[l]).at[0, MISC_F:MISC_END].set(b_fgate[l])
        qt, qit, k, vt, kidx, wt, gt, qm, km, vm, om = _in_proj(
            x, mod[l], row(mix_norm_pre[l]), win, row(q_latent_norm[l]), wq, conv_w[l], row(conv_b[l]), gb,
            cos, sin, tm, l)
        att = _dsa(qt, qit, k, vt, kidx, wt, attn_out_norm[l])
        hm = _mlstm(qm, km, vm, om, gt, mlstm_out_norm[l])
        x = _out_ffn(att, hm, x, mod[l], wout, row(mix_norm_post[l]), row(ffn_norm_pre[l]), wgu, wdn,
                     row(ffn_norm_post[l]), tm, l)
    return x
```
